```python
import math
import jax, jax.numpy as jnp
from jax import lax
import numpy as np

D_MODEL = 2048
BATCH = 4
SEQ = 2048
DEPTH = 2
DEC_BATCH = 128
DEC_SEQ = 1
PAST_LEN = 16384
PAGE_SIZE = 128

N_META = 16
CHUNK = 64
D_FF = 4 * D_MODEL
H_A = 4
W_A = D_MODEL // 2
DH_A = W_A // H_A
H_B = 8
W_B = D_MODEL // 2
DH_B = W_B // H_B
CONV_W = 4
DH_C = 64
W_C = D_MODEL // 2
H_C = W_C // DH_C
R_W = D_MODEL // 32
R_A = D_MODEL // 32
R_G = D_MODEL // 16
N_A = 4 * W_A + 2 * H_A
N_B = 4 * W_B + 2 * H_B
N_C = 3 * W_C + R_W + R_A + R_G
N_GATE = 3 * D_MODEL
N_IN = N_A + N_B + N_C + N_GATE
RMS_EPS = 1e-6
GN_EPS = 64e-5
L2_EPS = 1e-12

kernel_name = "hybrid_mlstm_gdn_rwkv7_step"


def _split(x, sizes):
    offs, o = [], 0
    for s in sizes[:-1]:
        o += s
        offs.append(o)
    return jnp.split(x, offs, axis=-1)


def _rms(x, w, eps=RMS_EPS):
    xf = x.astype(jnp.float32)
    y = xf * lax.rsqrt(jnp.mean(xf * xf, axis=-1, keepdims=True) + eps)
    return (y * w.astype(jnp.float32)).astype(x.dtype)


def _l2norm(x):
    return x * lax.rsqrt(jnp.sum(x * x, axis=-1, keepdims=True) + L2_EPS)


def _run_chunks(chunk_fn, state, xs, lead, chunk):
    ys = []
    if lead > 0:
        state, y = chunk_fn(state, tuple(x[:, :lead] for x in xs))
        ys.append(y)
    rest = xs[0].shape[1] - lead
    if rest > 0:
        n = rest // chunk
        blocks = tuple(jnp.swapaxes(x[:, lead:].reshape(x.shape[0], n, chunk, *x.shape[2:]), 0, 1) for x in xs)
        state, yb = lax.scan(chunk_fn, state, blocks)
        yb = jnp.swapaxes(yb, 0, 1)
        ys.append(yb.reshape(yb.shape[0], n * chunk, *yb.shape[3:]))
    return state, jnp.concatenate(ys, axis=1)


def _mlstm_chunk(state, xs):
    c, n, m = state
    q, k, v, logi, logf = xs
    L = q.shape[1]
    b = jnp.swapaxes(jnp.cumsum(logf, axis=1), 1, 2)
    li = jnp.swapaxes(logi, 1, 2)
    causal = jnp.tril(jnp.ones((L, L), dtype=bool))
    d_mat = jnp.where(causal, b[..., :, None] - b[..., None, :] + li[..., None, :], -jnp.inf)
    inter = b + m[..., None]
    m_row = jnp.maximum(inter, jnp.max(d_mat, axis=-1))
    w_intra = jnp.exp(d_mat - m_row[..., None])
    w_inter = jnp.exp(inter - m_row)
    s = jnp.einsum('blhd,bshd->bhls', q, k) * w_intra
    num = (jnp.einsum('bhls,bshd->blhd', s, v)
           + jnp.swapaxes(w_inter, 1, 2)[..., None] * jnp.einsum('blhk,bhkv->blhv', q, c))
    den = jnp.sum(s, axis=-1) + w_inter * jnp.einsum('blhk,bhk->bhl', q, n)
    den = jnp.maximum(jnp.abs(den), jnp.exp(-m_row))
    h = num / jnp.swapaxes(den, 1, 2)[..., None]
    b_last = b[..., -1]
    g = b_last[..., None] - b + li
    m_new = jnp.maximum(b_last + m, jnp.max(g, axis=-1))
    w_state = jnp.exp(g - m_new[..., None])
    decay = jnp.exp(b_last + m - m_new)
    c = decay[..., None, None] * c + jnp.einsum('bhs,bshk,bshv->bhkv', w_state, k, v)
    n = decay[..., None] * n + jnp.einsum('bhs,bshk->bhk', w_state, k)
    return (c, n, m_new), h


def _gdn_chunk(s, xs):
    q, k, v, log_alpha, beta = xs
    L = q.shape[1]
    q, k, v = (jnp.swapaxes(t, 1, 2) for t in (q, k, v))
    g = jnp.swapaxes(jnp.cumsum(log_alpha, axis=1), 1, 2)
    beta = jnp.swapaxes(beta, 1, 2)
    incl = jnp.tril(jnp.ones((L, L), dtype=bool))
    strict = jnp.tril(jnp.ones((L, L), dtype=bool), -1)
    decay = jnp.exp(jnp.where(incl, g[..., :, None] - g[..., None, :], -jnp.inf))
    a_mat = jnp.where(strict, beta[..., :, None] * jnp.einsum('bhld,bhsd->bhls', k, k) * decay, 0.0)
    eye = jnp.eye(L, dtype=q.dtype)
    rhs = jnp.concatenate([v * beta[..., None], k * (beta * jnp.exp(g))[..., None]], axis=-1)
    sol = lax.linalg.triangular_solve(a_mat + eye, rhs, left_side=True, lower=True, unit_diagonal=True)
    u, w = sol[..., :v.shape[-1]], sol[..., v.shape[-1]:]
    v_new = u - jnp.einsum('bhlk,bhkv->bhlv', w, s)
    attn = jnp.where(incl, jnp.einsum('bhld,bhsd->bhls', q, k) * decay, 0.0)
    o = (jnp.einsum('bhlk,bhkv->bhlv', q * jnp.exp(g)[..., None], s)
         + jnp.einsum('bhls,bhsv->bhlv', attn, v_new))
    g_last = g[..., -1]
    s = (jnp.exp(g_last)[..., None, None] * s
         + jnp.einsum('bhsk,bhsv->bhkv', k * jnp.exp(g_last[..., None] - g)[..., None], v_new))
    return s, jnp.swapaxes(o, 1, 2)


def _rwkv_step(s, xs):
    r, w, k, v, a_vec, b_vec = xs
    sa = jnp.einsum('bhvk,bhk->bhv', s, a_vec)
    s = s * w[:, :, None, :] + sa[..., None] * b_vec[:, :, None, :] + v[..., None] * k[:, :, None, :]
    return s, jnp.einsum('bhvk,bhk->bhv', s, r)


def _causal_conv(buf, u, w):
    T = u.shape[1]
    full = jnp.concatenate([buf, u], axis=1)
    y = sum(full[:, j:j + T] * w[j] for j in range(CONV_W))
    return y, full[:, T:]


def _mlstm_branch(pa, c0, n0, m0, b_i, b_f, norm_w, lead, chunk):
    B, T, _ = pa.shape
    q, k, v, o, ig, fg = _split(pa, (W_A, W_A, W_A, W_A, H_A, H_A))
    heads = lambda t: t.reshape(B, T, H_A, DH_A)
    logi = ig + b_i
    logf = jax.nn.log_sigmoid(fg + b_f)
    (c, n, m), h = _run_chunks(_mlstm_chunk, (c0, n0, m0),
                               (heads(q), heads(k) * DH_A ** -0.5, heads(v), logi, logf), lead, chunk)
    h = _rms(h, norm_w.reshape(H_A, DH_A))
    return jax.nn.sigmoid(o) * h.reshape(B, T, W_A), c, n, m


def _gdn_branch(pb, conv_buf, s0, conv_w, a_log, dt_bias, norm_w, lead, chunk):
    B, T, _ = pb.shape
    qkv, z, a, b = _split(pb, (3 * W_B, W_B, H_B, H_B))
    qkv, new_buf = _causal_conv(conv_buf, qkv, conv_w)
    q, k, v = [t.reshape(B, T, H_B, DH_B) for t in _split(jax.nn.silu(qkv), (W_B, W_B, W_B))]
    q = _l2norm(q) * DH_B ** -0.5
    k = _l2norm(k)
    log_alpha = -jnp.exp(a_log) * jax.nn.softplus(a + dt_bias)
    beta = jax.nn.sigmoid(b)
    s, o = _run_chunks(_gdn_chunk, s0, (q, k, v, log_alpha, beta), lead, chunk)
    o = _rms(o, norm_w) * jax.nn.silu(z.reshape(B, T, H_B, DH_B))
    return o.reshape(B, T, W_B), s, new_buf


def _rwkv_branch(pc, shift_buf, s0, mu, w0, w2, a0, a2, g2, k_k, k_a, r_k, ln_w, ln_b):
    B, T, _ = pc.shape
    prev = jnp.concatenate([shift_buf[:, None], pc[:, :-1]], axis=1)
    xm = pc + (prev - pc) * mu
    r, k, v, xw, xa, xg = _split(xm, (W_C, W_C, W_C, R_W, R_A, R_G))
    w_log = -jax.nn.softplus(-(w0 + jnp.tanh(xw) @ w2)) - 0.5
    decay = jnp.exp(-jnp.exp(w_log))
    a = jax.nn.sigmoid(a0 + xa @ a2)
    g = jax.nn.sigmoid(xg) @ g2
    heads = lambda t: t.reshape(B, T, H_C, DH_C)
    kk = _l2norm(heads(k * k_k))
    k = k * (1.0 + (a - 1.0) * k_a)
    r_h, k_h, v_h, a_h = heads(r), heads(k), heads(v), heads(a)
    tm = lambda t: jnp.swapaxes(t, 0, 1)
    s, y = lax.scan(_rwkv_step, s0, (tm(r_h), tm(heads(decay)), tm(k_h), tm(v_h), tm(-kk), tm(kk * a_h)))
    y = tm(y)
    mean = jnp.mean(y, axis=-1, keepdims=True)
    var = jnp.mean(jnp.square(y - mean), axis=-1, keepdims=True)
    y = (y - mean) * lax.rsqrt(var + GN_EPS) * ln_w.reshape(H_C, DH_C) + ln_b.reshape(H_C, DH_C)
    y = y + jnp.sum(r_h * k_h * r_k, axis=-1, keepdims=True) * v_h
    return y.reshape(B, T, W_C) * g, s, pc[:, -1]


def _hybrid_layer(x, st, lp, lead, chunk):
    f32 = lambda t: t.astype(jnp.float32)
    c0, n0, m0, gs0, gconv0, rs0, rshift0 = [f32(s) for s in st]
    B, T, _ = x.shape
    dt = x.dtype
    h = _rms(x, lp["norm_mix_w"])
    proj = f32(h @ lp["w_in"])
    pa, pb, pc, pg = _split(proj, (N_A, N_B, N_C, N_GATE))
    ya, c, n, m = _mlstm_branch(pa, c0, n0, m0, f32(lp["mlstm_b_i"]), f32(lp["mlstm_b_f"]),
                                f32(lp["mlstm_norm_w"]), lead, chunk)
    yb, gs, gconv = _gdn_branch(pb, gconv0, gs0, f32(lp["gdn_conv_w"]), f32(lp["gdn_a_log"]),
                                f32(lp["gdn_dt_bias"]), f32(lp["gdn_norm_w"]), lead, chunk)
    yc, rs, rshift = _rwkv_branch(pc, rshift0, rs0, f32(lp["rwkv_mu"]), f32(lp["rwkv_w0"]), f32(lp["rwkv_w2"]),
                                  f32(lp["rwkv_a0"]), f32(lp["rwkv_a2"]), f32(lp["rwkv_g2"]), f32(lp["rwkv_k_k"]),
                                  f32(lp["rwkv_k_a"]), f32(lp["rwkv_r_k"]), f32(lp["rwkv_ln_w"]), f32(lp["rwkv_ln_b"]))
    gates = jax.nn.sigmoid(pg).astype(dt).reshape(B, T, 3, D_MODEL)
    merged = (gates[:, :, 0] * (ya.astype(dt) @ lp["w_branch_a"])
              + gates[:, :, 1] * (yb.astype(dt) @ lp["w_branch_b"])
              + gates[:, :, 2] * (yc.astype(dt) @ lp["w_branch_c"]))
    x = x + merged @ lp["w_out"]
    u = jax.nn.relu(_rms(x, lp["norm_mlp_w"]) @ lp["w_up"])
    x = x + (u * u) @ lp["w_down"]
    return x, (c, n, m, gs, gconv, rs, rshift)


def _trunk(x, states, layers, final_norm_w, lead, chunk):
    new = [[] for _ in states]
    for l in range(DEPTH):
        lp = {name: p[l] for name, p in layers.items()}
        x, st = _hybrid_layer(x, [s[l] for s in states], lp, lead, chunk)
        for acc, s in zip(new, st):
            acc.append(s)
    return _rms(x, final_norm_w), [jnp.stack(acc) for acc in new]


def _zero_states(b):
    z = lambda *shape: jnp.zeros((DEPTH, b) + shape, jnp.float32)
    return [z(H_A, DH_A, DH_A), z(H_A, DH_A), z(H_A), z(H_B, DH_B, DH_B),
            z(CONV_W - 1, 3 * W_B), z(H_C, DH_C, DH_C), z(N_C)]


def setup_inputs(seed: int = 0) -> dict:
    key = jax.random.key(seed)
    keys = jax.random.split(key, 48)
    counter = iter(range(48))
    nxt = lambda: keys[next(counter)]
    nrm = lambda shape, scale=1.0: scale * jax.random.normal(nxt(), shape, jnp.float32)
    uni = lambda shape, lo, hi: jax.random.uniform(nxt(), shape, jnp.float32, lo, hi)
    gain = lambda shape: 1.0 + 0.02 * jax.random.normal(nxt(), shape, jnp.float32)
    L = DEPTH
    out = {
        "x_prompt": nrm((BATCH, SEQ, D_MODEL)),
        "x_sample": nrm((DEC_BATCH, DEC_SEQ, D_MODEL)),
        "state_mlstm_c": nrm((L, DEC_BATCH, H_A, DH_A, DH_A), 0.1),
        "state_mlstm_n": nrm((L, DEC_BATCH, H_A, DH_A)),
        "state_mlstm_m": nrm((L, DEC_BATCH, H_A), 0.5),
        "state_gdn_s": nrm((L, DEC_BATCH, H_B, DH_B, DH_B), 0.1),
        "state_gdn_conv": nrm((L, DEC_BATCH, CONV_W - 1, 3 * W_B)),
        "state_rwkv_s": nrm((L, DEC_BATCH, H_C, DH_C, DH_C), 0.1),
        "state_rwkv_shift": nrm((L, DEC_BATCH, N_C)),
        "meta_tokens": nrm((N_META, D_MODEL)),
        "norm_mix_w": gain((L, D_MODEL)),
        "w_in": nrm((L, D_MODEL, N_IN), D_MODEL ** -0.5),
        "mlstm_b_i": nrm((L, H_A), 0.1),
        "mlstm_b_f": 3.0 + nrm((L, H_A), 0.5),
        "mlstm_norm_w": gain((L, W_A)),
        "gdn_conv_w": nrm((L, CONV_W, 3 * W_B), CONV_W ** -0.5),
        "gdn_a_log": jnp.log(uni((L, H_B), 1.0, 16.0)),
        "gdn_dt_bias": jnp.log(jnp.expm1(jnp.exp(uni((L, H_B), math.log(1e-3), math.log(1e-1))))),
        "gdn_norm_w": gain((L, DH_B)),
        "rwkv_mu": uni((L, N_C), 0.0, 1.0),
        "rwkv_w0": uni((L, W_C), -6.0, 1.0),
        "rwkv_w2": nrm((L, R_W, W_C), 0.1 * R_W ** -0.5),
        "rwkv_a0": nrm((L, W_C), 0.1),
        "rwkv_a2": nrm((L, R_A, W_C), 0.5 * R_A ** -0.5),
        "rwkv_g2": nrm((L, R_G, W_C), R_G ** -0.5),
        "rwkv_k_k": 0.85 + nrm((L, W_C), 0.05),
        "rwkv_k_a": uni((L, W_C), 0.5, 1.0),
        "rwkv_r_k": nrm((L, H_C, DH_C), 0.1),
        "rwkv_ln_w": gain((L, W_C)),
        "rwkv_ln_b": nrm((L, W_C), 0.01),
        "w_branch_a": nrm((L, W_A, D_MODEL), W_A ** -0.5),
        "w_branch_b": nrm((L, W_B, D_MODEL), W_B ** -0.5),
        "w_branch_c": nrm((L, W_C, D_MODEL), W_C ** -0.5),
        "w_out": nrm((L, D_MODEL, D_MODEL), D_MODEL ** -0.5),
        "norm_mlp_w": gain((L, D_MODEL)),
        "w_up": nrm((L, D_MODEL, D_FF), D_MODEL ** -0.5),
        "w_down": nrm((L, D_FF, D_MODEL), D_FF ** -0.5),
        "final_norm_w": gain((D_MODEL,)),
    }
    return out


def reference(x_prompt, x_sample, state_mlstm_c, state_mlstm_n, state_mlstm_m, state_gdn_s, state_gdn_conv,
              state_rwkv_s, state_rwkv_shift, meta_tokens, norm_mix_w, w_in, mlstm_b_i, mlstm_b_f, mlstm_norm_w,
              gdn_conv_w, gdn_a_log, gdn_dt_bias, gdn_norm_w, rwkv_mu, rwkv_w0, rwkv_w2, rwkv_a0, rwkv_a2,
              rwkv_g2, rwkv_k_k, rwkv_k_a, rwkv_r_k, rwkv_ln_w, rwkv_ln_b, w_branch_a, w_branch_b, w_branch_c,
              w_out, norm_mlp_w, w_up, w_down, final_norm_w):
    layers = {
        "norm_mix_w": norm_mix_w, "w_in": w_in, "mlstm_b_i": mlstm_b_i, "mlstm_b_f": mlstm_b_f,
        "mlstm_norm_w": mlstm_norm_w, "gdn_conv_w": gdn_conv_w, "gdn_a_log": gdn_a_log,
        "gdn_dt_bias": gdn_dt_bias, "gdn_norm_w": gdn_norm_w, "rwkv_mu": rwkv_mu, "rwkv_w0": rwkv_w0,
        "rwkv_w2": rwkv_w2, "rwkv_a0": rwkv_a0, "rwkv_a2": rwkv_a2, "rwkv_g2": rwkv_g2,
        "rwkv_k_k": rwkv_k_k, "rwkv_k_a": rwkv_k_a, "rwkv_r_k": rwkv_r_k, "rwkv_ln_w": rwkv_ln_w,
        "rwkv_ln_b": rwkv_ln_b, "w_branch_a": w_branch_a, "w_branch_b": w_branch_b,
        "w_branch_c": w_branch_c, "w_out": w_out, "norm_mlp_w": norm_mlp_w, "w_up": w_up, "w_down": w_down,
    }
    b = x_prompt.shape[0]
    meta = jnp.broadcast_to(meta_tokens.astype(x_prompt.dtype)[None], (b, N_META, D_MODEL))
    xp = jnp.concatenate([meta, x_prompt], axis=1)
    yp, sp = _trunk(xp, _zero_states(b), layers, final_norm_w, N_META, CHUNK)
    y_prompt = yp[:, N_META:]
    y_sample, ss = _trunk(x_sample, [state_mlstm_c, state_mlstm_n, state_mlstm_m, state_gdn_s, state_gdn_conv,
                                     state_rwkv_s, state_rwkv_shift],
                          layers, final_norm_w, x_sample.shape[1], CHUNK)
    p_c, p_n, p_m, p_gs, p_gconv, p_rs, p_rshift = sp
    s_c, s_n, s_m, s_gs, s_gconv, s_rs, s_rshift = ss
    return (y_prompt, y_sample, p_c, p_n, p_m, p_gs, p_gconv, p_rs, p_rshift,
            s_c, s_n, s_m, s_gs, s_gconv, s_rs, s_rshift)
```

```python
import functools

import jax
import jax.numpy as jnp
from jax import lax
from jax.experimental import pallas as pl
from jax.experimental.pallas import tpu as pltpu

F32 = jnp.float32
_MXU_DT = jnp.bfloat16

D_MODEL = 2048
D_FF = 4 * D_MODEL
N_META = 16
CHUNK = 64
LEAD_PAD = CHUNK - N_META
H_A, DH_A, W_A = 4, 256, 1024
H_B, DH_B, W_B, CONV_W = 8, 128, 1024, 4
H_C, DH_C, W_C = 16, 64, 1024
R_W, R_A, R_G = 64, 64, 128
N_A = 4 * W_A + 2 * H_A
N_B = 4 * W_B + 2 * H_B
N_C = 3 * W_C + R_W + R_A + R_G
N_GATE = 3 * D_MODEL
RMS_EPS = 1e-6
GN_EPS = 64e-5
L2_EPS = 1e-12
NEG = -1e30

OFF_G = 0
OFF_A = OFF_G + N_GATE
OFF_B = OFF_A + 4 * W_A
OFF_C = OFF_B + 4 * W_B
OFF_S = OFF_C + N_C
N_PROJ = 17920
LANE = 128
TM_TARGET = 1100
VMEM_LIMIT = 56 * 1024 * 1024
SB = 16


def _cast(x):
    return x.astype(_MXU_DT)


def _mm(a, b):
    return jnp.dot(_cast(a), _cast(b), preferred_element_type=F32)


def _mm_nt(a, b):
    return lax.dot_general(_cast(a), _cast(b), (((1,), (1,)), ((), ())), preferred_element_type=F32)


def _mm_tn(a, b):
    return lax.dot_general(_cast(a), _cast(b), (((0,), (0,)), ((), ())), preferred_element_type=F32)


def _split(x):
    hi = x.astype(_MXU_DT)
    lo = (x - hi.astype(F32)).astype(_MXU_DT)
    return hi, lo


def _mm3(a, b):
    ah, al = _split(a)
    bh, bl = _split(b)
    return (jnp.dot(ah, bh, preferred_element_type=F32) + jnp.dot(ah, bl, preferred_element_type=F32)
            + jnp.dot(al, bh, preferred_element_type=F32))


def _mm2_lhs(a, b_exact):
    ah, al = _split(a)
    return jnp.dot(ah, b_exact, preferred_element_type=F32) + jnp.dot(al, b_exact, preferred_element_type=F32)


def _sigmoid(x):
    return 1.0 / (1.0 + jnp.exp(-x))


def _softplus(x):
    return jnp.maximum(x, 0.0) + jnp.log(1.0 + jnp.exp(-jnp.abs(x)))


def _log_sigmoid(x):
    return -_softplus(-x)


def _silu(x):
    return x * _sigmoid(x)


def _pick_tile(m, target, mult=16):
    best = None
    for t in range(mult, min(m, target) + 1, mult):
        if m % t == 0:
            best = t
    return m if best is None else best


def _cparams(*sem):
    return pltpu.CompilerParams(dimension_semantics=sem, vmem_limit_bytes=VMEM_LIMIT)


def _col_to_row(col):
    n = col.shape[0]
    return jnp.transpose(jnp.broadcast_to(col, (n, LANE)))[0:1, :]


def _pad_rows(x, n):
    if x.shape[0] == n:
        return x
    return jnp.concatenate([x, jnp.zeros((n - x.shape[0], x.shape[1]), x.dtype)], axis=0)


def _cumsum_rows(x):
    n = x.shape[0]
    row = lax.broadcasted_iota(jnp.int32, x.shape, 0)
    s = 1
    while s < n:
        x = x + jnp.where(row >= s, pltpu.roll(x, s, 0), 0.0)
        s *= 2
    return x


def _shift_rows(cur, prev, j):
    row = lax.broadcasted_iota(jnp.int32, cur.shape, 0)
    return jnp.where(row < j, pltpu.roll(prev, j, 0), pltpu.roll(cur, j, 0))


def _tri_inv(n_mat, eye, levels):
    t = eye + n_mat
    p = n_mat
    for _ in range(levels - 1):
        p = _mm3(p, p)
        t = t + _mm3(t, p)
    return t


def _mm_rms_body(x_ref, nw_ref, w_ref, o_ref, h_ref, *, act):
    @pl.when(pl.program_id(1) == 0)
    def _():
        x = x_ref[...]
        ms = jnp.mean(x * x, axis=-1, keepdims=True)
        h_ref[...] = (x * lax.rsqrt(ms + RMS_EPS) * nw_ref[...]).astype(h_ref.dtype)

    acc = jnp.dot(h_ref[...], w_ref[...], preferred_element_type=F32)
    if act == "relu2":
        acc = jnp.maximum(acc, 0.0)
        acc = acc * acc
    o_ref[...] = acc.astype(o_ref.dtype)


def _mm_rms(x, nw, w, out_dtype, act, tn, name):
    m, k = x.shape
    n = w.shape[1]
    tm = _pick_tile(m, TM_TARGET)
    return pl.pallas_call(
        functools.partial(_mm_rms_body, act=act),
        grid=(m // tm, n // tn),
        in_specs=[pl.BlockSpec((tm, k), lambda i, j: (i, 0)),
                  pl.BlockSpec((1, k), lambda i, j: (0, 0)),
                  pl.BlockSpec((k, tn), lambda i, j: (0, j))],
        out_specs=pl.BlockSpec((tm, tn), lambda i, j: (i, j)),
        out_shape=jax.ShapeDtypeStruct((m, n), out_dtype),
        scratch_shapes=[pltpu.VMEM((tm, k), _MXU_DT)],
        compiler_params=_cparams("parallel", "arbitrary"),
        name=name,
    )(x, nw, w)


def _mm_merge_body(ya_ref, yb_ref, yc_ref, wa_ref, wb_ref, wc_ref, g0_ref, g1_ref, g2_ref, o_ref):
    acc = _sigmoid(g0_ref[...]) * jnp.dot(ya_ref[...], wa_ref[...], preferred_element_type=F32)
    acc += _sigmoid(g1_ref[...]) * jnp.dot(yb_ref[...], wb_ref[...], preferred_element_type=F32)
    acc += _sigmoid(g2_ref[...]) * jnp.dot(yc_ref[...], wc_ref[...], preferred_element_type=F32)
    o_ref[...] = acc.astype(o_ref.dtype)


def _mm_merge(ya, yb, yc, wa, wb, wc, proj, tn=512):
    m, k = ya.shape
    n = wa.shape[1]
    tm = _pick_tile(m, TM_TARGET)
    nj = n // tn
    y_spec = pl.BlockSpec((tm, k), lambda i, j: (i, 0))
    w_spec = pl.BlockSpec((k, tn), lambda i, j: (0, j))
    g_specs = [pl.BlockSpec((tm, tn), functools.partial(lambda i, j, g: (i, g * nj + j), g=g)) for g in range(3)]
    return pl.pallas_call(
        _mm_merge_body,
        grid=(m // tm, nj),
        in_specs=[y_spec, y_spec, y_spec, w_spec, w_spec, w_spec] + g_specs,
        out_specs=pl.BlockSpec((tm, tn), lambda i, j: (i, j)),
        out_shape=jax.ShapeDtypeStruct((m, n), _MXU_DT),
        compiler_params=_cparams("parallel", "parallel"),
        name="mm_merge",
    )(ya, yb, yc, wa, wb, wc, proj, proj, proj)


def _mm_res_body(a_ref, w_ref, x_ref, keep_ref, o_ref):
    k = pl.program_id(2)

    @pl.when(k == 0)
    def _():
        o_ref[...] = x_ref[...]

    o_ref[...] += jnp.dot(a_ref[...], w_ref[...], preferred_element_type=F32)

    @pl.when(k == pl.num_programs(2) - 1)
    def _():
        o_ref[...] = jnp.where(keep_ref[...] > 0.0, o_ref[...], 0.0)


def _mm_res(a, w, x, keep, tn=1024, tk=2048, name="mm_res"):
    m, kk = a.shape
    n = w.shape[1]
    tm = _pick_tile(m, TM_TARGET)
    return pl.pallas_call(
        _mm_res_body,
        grid=(m // tm, n // tn, kk // tk),
        in_specs=[pl.BlockSpec((tm, tk), lambda i, j, k: (i, k)),
                  pl.BlockSpec((tk, tn), lambda i, j, k: (k, j)),
                  pl.BlockSpec((tm, tn), lambda i, j, k: (i, j)),
                  pl.BlockSpec((tm, 1), lambda i, j, k: (i, 0))],
        out_specs=pl.BlockSpec((tm, tn), lambda i, j, k: (i, j)),
        out_shape=jax.ShapeDtypeStruct((m, n), F32),
        compiler_params=_cparams("parallel", "parallel", "arbitrary"),
        name=name,
    )(a, w, x, keep)


def _rms_body(x_ref, w_ref, o_ref):
    x = x_ref[...]
    ms = jnp.mean(x * x, axis=-1, keepdims=True)
    o_ref[...] = x * lax.rsqrt(ms + RMS_EPS) * w_ref[...]


def _final_rms(x, w):
    m, d = x.shape
    tm = _pick_tile(m, TM_TARGET)
    return pl.pallas_call(
        _rms_body,
        grid=(m // tm,),
        in_specs=[pl.BlockSpec((tm, d), lambda i: (i, 0)), pl.BlockSpec((1, d), lambda i: (0, 0))],
        out_specs=pl.BlockSpec((tm, d), lambda i: (i, 0)),
        out_shape=jax.ShapeDtypeStruct((m, d), F32),
        compiler_params=_cparams("parallel"),
        name="final_rms",
    )(x, w)


def _mlstm_prompt_body(q_ref, k_ref, v_ref, o_ref, sm_ref, bias_ref, nw_ref,
                       ya_ref, c_ref, n_ref, m_ref):
    ci = pl.program_id(1)

    @pl.when(ci == 0)
    def _():
        c_ref[...] = jnp.zeros_like(c_ref)
        n_ref[...] = jnp.zeros_like(n_ref)
        m_ref[...] = jnp.zeros_like(m_ref)

    L = CHUNK
    row1 = lax.broadcasted_iota(jnp.int32, (L, 1), 0)
    is_pad = jnp.logical_and(ci == 0, row1 < LEAD_PAD)
    smb = sm_ref[...] + bias_ref[...]
    li_all = jnp.where(is_pad, NEG, smb)
    lf_all = jnp.where(is_pad, 0.0, _log_sigmoid(smb))
    li_t = jnp.transpose(_pad_rows(li_all, LANE))
    lf_t = jnp.transpose(_pad_rows(lf_all, LANE))
    rr = lax.broadcasted_iota(jnp.int32, (L, L), 0)
    cc = lax.broadcasted_iota(jnp.int32, (L, L), 1)
    tril = cc <= rr
    for h in range(H_A):
        sl = slice(h * DH_A, (h + 1) * DH_A)
        li_col = li_all[:, h:h + 1]
        lf_col = lf_all[:, H_A + h:H_A + h + 1]
        li_row = li_t[h:h + 1, :L]
        lf_row = lf_t[H_A + h:H_A + h + 1, :L]
        b_col = jnp.sum(jnp.where(tril, lf_row, 0.0), axis=1, keepdims=True)
        b_row = jnp.sum(jnp.where(rr <= cc, lf_col, 0.0), axis=0, keepdims=True)
        d = jnp.where(tril, b_col - b_row + li_row, NEG)
        m_prev = m_ref[0, h, 0:1, 0:1]
        inter = b_col + m_prev
        m_row = jnp.maximum(inter, jnp.max(d, axis=1, keepdims=True))
        w_intra = jnp.exp(d - m_row)
        w_inter = jnp.exp(inter - m_row)
        qh = q_ref[:, sl]
        kh = k_ref[:, sl] * (DH_A ** -0.5)
        vh = v_ref[:, sl]
        s = _mm_nt(qh, kh) * w_intra
        c_h = c_ref[0, h]
        n_h = n_ref[0, h]
        num = _mm(s, vh) + w_inter * _mm(qh, c_h)
        den = jnp.sum(s, axis=1, keepdims=True) + w_inter * jnp.sum(qh * n_h, axis=1, keepdims=True)
        den = jnp.maximum(jnp.abs(den), jnp.exp(-m_row))
        hh = num / den
        hh = hh * lax.rsqrt(jnp.mean(hh * hh, axis=-1, keepdims=True) + RMS_EPS) * nw_ref[:, sl]
        ya_ref[:, sl] = (_sigmoid(o_ref[:, sl]) * hh).astype(ya_ref.dtype)
        b_last = b_col[L - 1:L, :]
        g_col = b_last - b_col + li_col
        m_new = jnp.maximum(b_last + m_prev, jnp.max(g_col, axis=0, keepdims=True))
        kw = kh * jnp.exp(g_col - m_new)
        decay = jnp.exp(b_last + m_prev - m_new)
        c_ref[0, h] = decay * c_h + _mm_tn(kw, vh)
        n_ref[0, h] = decay * n_h + jnp.sum(kw, axis=0, keepdims=True)
        m_ref[0, h] = jnp.broadcast_to(m_new, (8, LANE))


def _mlstm_prompt(proj, bias_row, nw, bsz, nc):
    ntp = bsz * nc * CHUNK
    col = lambda j: (lambda b, c: (b * nc + c, j))
    wide = OFF_A // W_A
    return pl.pallas_call(
        _mlstm_prompt_body,
        grid=(bsz, nc),
        in_specs=[pl.BlockSpec((CHUNK, W_A), col(wide)), pl.BlockSpec((CHUNK, W_A), col(wide + 1)),
                  pl.BlockSpec((CHUNK, W_A), col(wide + 2)), pl.BlockSpec((CHUNK, W_A), col(wide + 3)),
                  pl.BlockSpec((CHUNK, LANE), col(OFF_S // LANE)),
                  pl.BlockSpec((1, LANE), lambda b, c: (0, 0)),
                  pl.BlockSpec((1, W_A), lambda b, c: (0, 0))],
        out_specs=[pl.BlockSpec((CHUNK, W_A), lambda b, c: (b * nc + c, 0)),
                   pl.BlockSpec((1, H_A, DH_A, DH_A), lambda b, c: (b, 0, 0, 0)),
                   pl.BlockSpec((1, H_A, 1, DH_A), lambda b, c: (b, 0, 0, 0)),
                   pl.BlockSpec((1, H_A, 8, LANE), lambda b, c: (b, 0, 0, 0))],
        out_shape=[jax.ShapeDtypeStruct((ntp, W_A), _MXU_DT),
                   jax.ShapeDtypeStruct((bsz, H_A, DH_A, DH_A), F32),
                   jax.ShapeDtypeStruct((bsz, H_A, 1, DH_A), F32),
                   jax.ShapeDtypeStruct((bsz, H_A, 8, LANE), F32)],
        compiler_params=_cparams("parallel", "arbitrary"),
        name="mlstm_prompt",
    )(proj, proj, proj, proj, proj, bias_row, nw)


def _lane_pick(x, idx):
    lane = lax.broadcasted_iota(jnp.int32, x.shape, 1)
    return jnp.sum(jnp.where(lane == idx, x, 0.0), axis=1, keepdims=True)


def _mlstm_sample_body(q_ref, k_ref, v_ref, o_ref, sm_ref, bias_ref, nw_ref, c_ref, n_ref, m_ref,
                       ya_ref, co_ref, no_ref, mo_ref):
    h = pl.program_id(1)
    smb = sm_ref[...] + bias_ref[...]
    li = _lane_pick(smb, h)
    lf = _lane_pick(_log_sigmoid(smb), H_A + h)
    m_prev = m_ref[0]
    inter = lf + m_prev
    m_new = jnp.maximum(inter, li)
    w_intra = jnp.exp(li - m_new)
    w_inter = jnp.exp(inter - m_new)
    q = q_ref[...]
    k = k_ref[...] * (DH_A ** -0.5)
    v = v_ref[...]
    s = jnp.sum(q * k, axis=1, keepdims=True) * w_intra
    q_t = jnp.transpose(_pad_rows(q, LANE))
    k_t = jnp.transpose(_pad_rows(k, LANE))
    mo_ref[0] = m_new
    nw = nw_ref[...]
    for b in range(SB):
        rb = slice(b, b + 1)
        c_b = c_ref[0, b, 0]
        n_b = n_ref[b, 0]
        q_col = q_t[:, rb]
        k_col = k_t[:, rb]
        qc = jnp.sum(q_col * c_b, axis=0, keepdims=True)
        qn = jnp.sum(q[rb] * n_b, axis=1, keepdims=True)
        num = s[rb] * v[rb] + w_inter[rb] * qc
        den = s[rb] + w_inter[rb] * qn
        den = jnp.maximum(jnp.abs(den), jnp.exp(-m_new[rb]))
        hh = num / den
        hh = hh * lax.rsqrt(jnp.mean(hh * hh, axis=-1, keepdims=True) + RMS_EPS) * nw
        ya_ref[rb, :] = (_sigmoid(o_ref[rb, :]) * hh).astype(ya_ref.dtype)
        co_ref[b, 0] = w_inter[rb] * c_b + (k_col * w_intra[rb]) * v[rb]
        no_ref[b, 0] = w_inter[rb] * n_b + w_intra[rb] * k[rb]


def _mlstm_sample(proj, bias_row, nw, c_st, n_st, m_st, layer, ntp, nb):
    r0 = ntp // SB
    wide = OFF_A // DH_A
    col = lambda j: (lambda i, h: (r0 + i, j + h))
    dec_b = nb * SB
    return pl.pallas_call(
        _mlstm_sample_body,
        grid=(nb, H_A),
        in_specs=[pl.BlockSpec((SB, DH_A), col(wide)), pl.BlockSpec((SB, DH_A), col(wide + H_A)),
                  pl.BlockSpec((SB, DH_A), col(wide + 2 * H_A)), pl.BlockSpec((SB, DH_A), col(wide + 3 * H_A)),
                  pl.BlockSpec((SB, LANE), lambda i, h: (r0 + i, OFF_S // LANE)),
                  pl.BlockSpec((1, LANE), lambda i, h: (0, 0)),
                  pl.BlockSpec((1, DH_A), lambda i, h: (0, h)),
                  pl.BlockSpec((1, SB, 1, DH_A, DH_A), lambda i, h: (layer, i, h, 0, 0)),
                  pl.BlockSpec((None, SB, 1, 1, DH_A), lambda i, h: (layer, i, h, 0, 0)),
                  pl.BlockSpec((None, 1, SB, 1), lambda i, h: (layer, h, i, 0))],
        out_specs=[pl.BlockSpec((SB, DH_A), lambda i, h: (i, h)),
                   pl.BlockSpec((SB, 1, DH_A, DH_A), lambda i, h: (i, h, 0, 0)),
                   pl.BlockSpec((SB, 1, 1, DH_A), lambda i, h: (i, h, 0, 0)),
                   pl.BlockSpec((1, SB, 1), lambda i, h: (h, i, 0))],
        out_shape=[jax.ShapeDtypeStruct((dec_b, W_A), _MXU_DT),
                   jax.ShapeDtypeStruct((dec_b, H_A, DH_A, DH_A), F32),
                   jax.ShapeDtypeStruct((dec_b, H_A, 1, DH_A), F32),
                   jax.ShapeDtypeStruct((H_A, dec_b, 1), F32)],
        compiler_params=_cparams("parallel", "parallel"),
        name="mlstm_sample",
    )(proj, proj, proj, proj, proj, bias_row, nw, c_st, n_st, m_st)


GDN_G = 2
GDN_VT = GDN_G * CHUNK
GDN_W = GDN_G * DH_B
GDN_NG = H_B // GDN_G


def _gdn_gates(sm, alog_row, dtb_row):
    la = -jnp.exp(alog_row) * _softplus(sm + dtb_row)
    beta = _sigmoid(sm)
    return la, beta


def _gdn_post(o, z, nw):
    outs = []
    for h in range(H_B):
        sl = slice(h * DH_B, (h + 1) * DH_B)
        oh = o[:, sl]
        oh = oh * lax.rsqrt(jnp.mean(oh * oh, axis=-1, keepdims=True) + RMS_EPS) * nw
        outs.append(oh * _silu(z[:, sl]))
    return jnp.concatenate(outs, axis=1)


def _l2n_heads(x, scale):
    outs = []
    for h in range(H_B):
        xh = x[:, h * DH_B:(h + 1) * DH_B]
        outs.append(xh * (lax.rsqrt(jnp.sum(xh * xh, axis=-1, keepdims=True) + L2_EPS) * scale))
    return jnp.concatenate(outs, axis=1)


def _gdn_prompt_body(q_ref, k_ref, v_ref, z_ref, sm_ref, cw_ref, alog_ref, dtb_ref, nw_ref,
                     yb_ref, so_ref, sbd_ref, prev_ref):
    ci = pl.program_id(1)

    @pl.when(ci == 0)
    def _():
        sbd_ref[...] = jnp.zeros_like(sbd_ref)
        prev_ref[...] = jnp.zeros_like(prev_ref)

    L = CHUNK
    acts = []
    for p, ref in enumerate((q_ref, k_ref, v_ref)):
        u = ref[...]
        pu = prev_ref[p]
        cw = cw_ref[:, p * W_B:(p + 1) * W_B]
        y = cw[CONV_W - 1:CONV_W, :] * u
        for j in range(1, CONV_W):
            y = y + cw[CONV_W - 1 - j:CONV_W - j, :] * _shift_rows(u, pu, j)
        prev_ref[p] = u
        acts.append(_silu(y))
    qn = _l2n_heads(acts[0], DH_B ** -0.5)
    kn = _l2n_heads(acts[1], 1.0)
    va = acts[2]

    la_all, beta_all = _gdn_gates(sm_ref[...], alog_ref[...], dtb_ref[...])
    gc_all = _cumsum_rows(la_all)

    vt, w = GDN_VT, GDN_W
    ri = lax.broadcasted_iota(jnp.int32, (vt, vt), 0)
    cj = lax.broadcasted_iota(jnp.int32, (vt, vt), 1)
    same = (ri // L) == (cj // L)
    incl = jnp.logical_and(same, cj <= ri)
    strict = jnp.logical_and(same, cj < ri)
    eye = (ri == cj).astype(F32)
    bm = (lax.broadcasted_iota(jnp.int32, (vt, w), 0) // L) == (lax.broadcasted_iota(jnp.int32, (vt, w), 1) // DH_B)
    stack = lambda x: jnp.where(bm, jnp.concatenate([x] * GDN_G, axis=0), 0.0)

    o_parts = []
    for gi in range(GDN_NG):
        heads = [gi * GDN_G + e for e in range(GDN_G)]
        lsl = slice(gi * w, (gi + 1) * w)
        g_col = jnp.concatenate([gc_all[:, 8 + h:9 + h] for h in heads], axis=0)
        beta_col = jnp.concatenate([beta_all[:, 16 + h:17 + h] for h in heads], axis=0)
        glast_col = jnp.concatenate(
            [jnp.broadcast_to(gc_all[L - 1:L, 8 + h:9 + h], (L, 1)) for h in heads], axis=0)
        g_row = _col_to_row(g_col)
        k_s = stack(kn[:, lsl])
        q_s = stack(qn[:, lsl])
        v_s = stack(va[:, lsl])
        dec = jnp.exp(jnp.where(incl, g_col - g_row, NEG))
        a_mat = jnp.where(strict, beta_col * _mm_nt(k_s, k_s) * dec, 0.0)
        t_inv = _tri_inv(-a_mat, eye, 6)
        eg_col = jnp.exp(g_col)
        rhs = jnp.concatenate([v_s * beta_col, k_s * (beta_col * eg_col)], axis=1)
        sol = _mm(t_inv, rhs)
        sbd = sbd_ref[gi]
        v_new = sol[:, :w] - _mm(sol[:, w:], sbd)
        attn = jnp.where(incl, _mm_nt(q_s, k_s) * dec, 0.0)
        o_m = _mm(q_s * eg_col, sbd) + _mm(attn, v_new)
        o_g = o_m[0:L]
        for e in range(1, GDN_G):
            o_g = o_g + o_m[e * L:(e + 1) * L]
        o_parts.append(o_g)
        eg_state = jnp.concatenate(
            [jnp.broadcast_to(jnp.exp(gc_all[L - 1:L, 8 + h:9 + h]), (DH_B, 1)) for h in heads], axis=0)
        sbd_ref[gi] = eg_state * sbd + _mm_tn(k_s * jnp.exp(glast_col - g_col), v_new)

    o = jnp.concatenate(o_parts, axis=1)
    yb_ref[...] = _gdn_post(o, z_ref[...], nw_ref[...]).astype(yb_ref.dtype)

    @pl.when(ci == pl.num_programs(1) - 1)
    def _():
        for h in range(H_B):
            gi, e = divmod(h, GDN_G)
            so_ref[0, h] = sbd_ref[gi, e * DH_B:(e + 1) * DH_B, e * DH_B:(e + 1) * DH_B]


def _gdn_prompt(proj, conv_w, alog_row, dtb_row, nw, bsz, nc):
    ntp = bsz * nc * CHUNK
    col = lambda j: (lambda b, c: (b * nc + c, j))
    wide = OFF_B // W_B
    const = lambda b, c: (0, 0)
    return pl.pallas_call(
        _gdn_prompt_body,
        grid=(bsz, nc),
        in_specs=[pl.BlockSpec((CHUNK, W_B), col(wide)), pl.BlockSpec((CHUNK, W_B), col(wide + 1)),
                  pl.BlockSpec((CHUNK, W_B), col(wide + 2)), pl.BlockSpec((CHUNK, W_B), col(wide + 3)),
                  pl.BlockSpec((CHUNK, LANE), col(OFF_S // LANE)),
                  pl.BlockSpec((CONV_W, 3 * W_B), const),
                  pl.BlockSpec((1, LANE), const), pl.BlockSpec((1, LANE), const),
                  pl.BlockSpec((1, DH_B), const)],
        out_specs=[pl.BlockSpec((CHUNK, W_B), lambda b, c: (b * nc + c, 0)),
                   pl.BlockSpec((1, H_B, DH_B, DH_B), lambda b, c: (b, 0, 0, 0))],
        out_shape=[jax.ShapeDtypeStruct((ntp, W_B), _MXU_DT),
                   jax.ShapeDtypeStruct((bsz, H_B, DH_B, DH_B), F32)],
        scratch_shapes=[pltpu.VMEM((GDN_NG, GDN_W, GDN_W), F32),
                        pltpu.VMEM((3, CHUNK, W_B), F32)],
        compiler_params=_cparams("parallel", "arbitrary"),
        name="gdn_prompt",
    )(proj, proj, proj, proj, proj, conv_w, alog_row, dtb_row, nw)


def _gdn_sample_body(q_ref, k_ref, v_ref, z_ref, sm_ref, bq_ref, bk_ref, bv_ref, cwq_ref, cwk_ref, cwv_ref,
                     alog_ref, dtb_ref, nw_ref, s_ref,
                     yb_ref, so_ref, nq_ref, nk_ref, nv_ref):
    h = pl.program_id(1)
    acts = []
    for u_ref, b_ref, cw_ref, n_ref in ((q_ref, bq_ref, cwq_ref, nq_ref), (k_ref, bk_ref, cwk_ref, nk_ref),
                                        (v_ref, bv_ref, cwv_ref, nv_ref)):
        u = u_ref[...]
        y = cw_ref[CONV_W - 1:CONV_W, :] * u
        for j in range(CONV_W - 1):
            y = y + cw_ref[j:j + 1, :] * b_ref[j]
        for j in range(CONV_W - 2):
            n_ref[j] = b_ref[j + 1]
        n_ref[CONV_W - 2] = u
        acts.append(_silu(y))
    qa, ka, va = acts
    q = qa * (lax.rsqrt(jnp.sum(qa * qa, axis=-1, keepdims=True) + L2_EPS) * DH_B ** -0.5)
    k = ka * lax.rsqrt(jnp.sum(ka * ka, axis=-1, keepdims=True) + L2_EPS)
    la_all, beta_all = _gdn_gates(sm_ref[...], alog_ref[...], dtb_ref[...])
    eg = jnp.exp(_lane_pick(la_all, 8 + h))
    beta = _lane_pick(beta_all, 16 + h)
    qk = jnp.sum(q * k, axis=1, keepdims=True)
    q_t = jnp.transpose(_pad_rows(q, LANE))
    k_t = jnp.transpose(_pad_rows(k, LANE))
    nw = nw_ref[...]
    z = z_ref[...]
    for b in range(SB):
        rb = slice(b, b + 1)
        s_b = s_ref[0, b, 0]
        k_col = k_t[:, rb]
        q_col = q_t[:, rb]
        ks = jnp.sum(k_col * s_b, axis=0, keepdims=True)
        qs = jnp.sum(q_col * s_b, axis=0, keepdims=True)
        v_new = beta[rb] * va[rb] - (beta[rb] * eg[rb]) * ks
        o = eg[rb] * qs + qk[rb] * v_new
        o = o * lax.rsqrt(jnp.mean(o * o, axis=-1, keepdims=True) + RMS_EPS) * nw
        yb_ref[rb, :] = (o * _silu(z[rb])).astype(yb_ref.dtype)
        so_ref[b, 0] = eg[rb] * s_b + k_col * v_new


def _gdn_sample(proj, conv_t, conv_w, alog_row, dtb_row, nw, s_st, layer, ntp, nb):
    r0 = ntp // SB
    base = OFF_B // DH_B
    col = lambda j: (lambda i, h: (r0 + i, base + j + h))
    const = lambda i, h: (0, 0)
    dec_b = nb * SB
    buf = lambda p: pl.BlockSpec((None, CONV_W - 1, SB, DH_B), lambda i, h: (layer, 0, i, p * H_B + h))
    cws = lambda p: pl.BlockSpec((CONV_W, DH_B), lambda i, h: (0, p * H_B + h))
    nbuf = pl.BlockSpec((CONV_W - 1, SB, DH_B), lambda i, h: (0, i, h))
    nshape = jax.ShapeDtypeStruct((CONV_W - 1, dec_b, W_B), F32)
    return pl.pallas_call(
        _gdn_sample_body,
        grid=(nb, H_B),
        in_specs=[pl.BlockSpec((SB, DH_B), col(0)), pl.BlockSpec((SB, DH_B), col(H_B)),
                  pl.BlockSpec((SB, DH_B), col(2 * H_B)), pl.BlockSpec((SB, DH_B), col(3 * H_B)),
                  pl.BlockSpec((SB, LANE), lambda i, h: (r0 + i, OFF_S // LANE)),
                  buf(0), buf(1), buf(2), cws(0), cws(1), cws(2),
                  pl.BlockSpec((1, LANE), const), pl.BlockSpec((1, LANE), const), pl.BlockSpec((1, DH_B), const),
                  pl.BlockSpec((1, SB, 1, DH_B, DH_B), lambda i, h: (layer, i, h, 0, 0))],
        out_specs=[pl.BlockSpec((SB, DH_B), lambda i, h: (i, h)),
                   pl.BlockSpec((SB, 1, DH_B, DH_B), lambda i, h: (i, h, 0, 0)),
                   nbuf, nbuf, nbuf],
        out_shape=[jax.ShapeDtypeStruct((dec_b, W_B), _MXU_DT),
                   jax.ShapeDtypeStruct((dec_b, H_B, DH_B, DH_B), F32),
                   nshape, nshape, nshape],
        compiler_params=_cparams("parallel", "parallel"),
        name="gdn_sample",
    )(proj, proj, proj, proj, proj, conv_t, conv_t, conv_t, conv_w, conv_w, conv_w,
      alog_row, dtb_row, nw, s_st)


RW_G = 4
RW_VT = RW_G * CHUNK
RW_W = RW_G * DH_C
RW_NG = H_C // RW_G


def _block_ones(n, seg):
    r = lax.broadcasted_iota(jnp.int32, (n, n), 0) // seg
    c = lax.broadcasted_iota(jnp.int32, (n, n), 1) // seg
    return (r == c).astype(_MXU_DT)


def _seg_sum(x, ones_blk):
    w = ones_blk.shape[0]
    parts = [_mm2_lhs(x[:, j * w:(j + 1) * w], ones_blk) for j in range(x.shape[1] // w)]
    return jnp.concatenate(parts, axis=1)


def _rwkv_pre(r_raw, k_raw, v_raw, l_raw, pr, pk, pv, pl_, prm, ones_blk):
    mu, w0, w2, a0, a2, g2, k_k, k_a = prm
    mix = lambda x, p, m: x + (p - x) * m
    r = mix(r_raw, pr, mu[:, 0:W_C])
    k = mix(k_raw, pk, mu[:, W_C:2 * W_C])
    v = mix(v_raw, pv, mu[:, 2 * W_C:3 * W_C])
    lo = mix(l_raw, pl_, mu[:, 3 * W_C:N_C])
    xw = lo[:, 0:R_W + R_A]
    xg = lo[:, R_W + R_A:R_W + R_A + R_G]
    lane = lax.broadcasted_iota(jnp.int32, xw.shape, 1)
    tw = jnp.where(lane < R_W, jnp.tanh(xw), 0.0)
    ta = jnp.where(lane < R_W, 0.0, xw)
    w_log = -_softplus(-(w0 + _mm(tw, w2))) - 0.5
    lw = -jnp.exp(w_log)
    a = _sigmoid(a0 + _mm(ta, a2))
    g = _mm(_sigmoid(xg), g2)
    kk = k * k_k
    kk = kk * lax.rsqrt(_seg_sum(kk * kk, ones_blk) + L2_EPS)
    k2 = k * (1.0 + (a - 1.0) * k_a)
    return r, lw, k2, v, -kk, kk * a, g


def _rwkv_post(y, r, k2, v, g, r_k, ln_w, ln_b, ones_blk):
    mean = _seg_sum(y, ones_blk) * (1.0 / DH_C)
    yc = y - mean
    var = _seg_sum(yc * yc, ones_blk) * (1.0 / DH_C)
    yn = yc * lax.rsqrt(var + GN_EPS) * ln_w + ln_b
    yn = yn + _seg_sum(r * k2 * r_k, ones_blk) * v
    return yn * g


def _rwkv_prompt_body(r_ref, k_ref, v_ref, l_ref, mu_ref, w0_ref, w2_ref, a0_ref, a2_ref, g2_ref,
                      kk_ref, ka_ref, rk_ref, lnw_ref, lnb_ref,
                      yc_ref, so_ref, sbd_ref, pr_ref, pk_ref, pv_ref, pl_ref):
    ci = pl.program_id(1)

    @pl.when(ci == 0)
    def _():
        sbd_ref[...] = jnp.zeros_like(sbd_ref)
        pr_ref[...] = jnp.zeros_like(pr_ref)
        pk_ref[...] = jnp.zeros_like(pk_ref)
        pv_ref[...] = jnp.zeros_like(pv_ref)
        pl_ref[...] = jnp.zeros_like(pl_ref)

    L = CHUNK
    ones_blk = _block_ones(RW_W, DH_C)
    raws = [ref[...] for ref in (r_ref, k_ref, v_ref, l_ref)]
    prevs = [_shift_rows(x, p[...], 1) for x, p in zip(raws, (pr_ref, pk_ref, pv_ref, pl_ref))]
    for x, p in zip(raws, (pr_ref, pk_ref, pv_ref, pl_ref)):
        p[...] = x
    prm = (mu_ref[...], w0_ref[...], w2_ref[...], a0_ref[...], a2_ref[...], g2_ref[...], kk_ref[...], ka_ref[...])
    r, lw, k2, v, a_vec, b_vec, g = _rwkv_pre(*raws, *prevs, prm, ones_blk)

    cl = _cumsum_rows(lw)
    e_incl = jnp.exp(cl)
    e_excl = jnp.exp(cl - lw)
    e_inv = jnp.exp(-cl)
    a_t = a_vec * e_excl
    r_t = r * e_incl
    b_t = b_vec * e_inv
    k_t = k2 * e_inv

    vt, w = RW_VT, RW_W
    ri = lax.broadcasted_iota(jnp.int32, (vt, vt), 0)
    cj = lax.broadcasted_iota(jnp.int32, (vt, vt), 1)
    same = (ri // L) == (cj // L)
    incl = jnp.logical_and(same, cj <= ri)
    strict = jnp.logical_and(same, cj < ri)
    eye = (ri == cj).astype(F32)
    bm = (lax.broadcasted_iota(jnp.int32, (vt, w), 0) // L) == (lax.broadcasted_iota(jnp.int32, (vt, w), 1) // DH_C)
    stack = lambda x: jnp.where(bm, jnp.concatenate([x] * RW_G, axis=0), 0.0)

    y_parts = []
    for gi in range(RW_NG):
        lsl = slice(gi * w, (gi + 1) * w)
        a_s, r_s, b_s, k_s, v_m = (stack(x[:, lsl]) for x in (a_t, r_t, b_t, k_t, v))
        sbd = sbd_ref[gi]
        ar = jnp.concatenate([a_s, r_s], axis=0)
        bk = jnp.concatenate([b_s, k_s], axis=0)
        gm = _mm_nt(ar, bk)
        ars = _mm_nt(ar, sbd)
        t_inv = _tri_inv(jnp.where(strict, gm[:vt, :vt], 0.0), eye, 6)
        u_m = _mm(t_inv, ars[:vt] + _mm(jnp.where(strict, gm[:vt, vt:], 0.0), v_m))
        uv = jnp.concatenate([u_m, v_m], axis=0)
        rbk = jnp.concatenate([jnp.where(incl, gm[vt:, :vt], 0.0), jnp.where(incl, gm[vt:, vt:], 0.0)], axis=1)
        y_m = ars[vt:] + _mm(rbk, uv)
        y_g = y_m[0:L]
        for e in range(1, RW_G):
            y_g = y_g + y_m[e * L:(e + 1) * L]
        y_parts.append(y_g)
        sbd_ref[gi] = (sbd + _mm_tn(uv, bk)) * e_incl[L - 1:L, lsl]

    y = jnp.concatenate(y_parts, axis=1)
    out = _rwkv_post(y, r, k2, v, g, rk_ref[...], lnw_ref[...], lnb_ref[...], ones_blk)
    yc_ref[...] = out.astype(yc_ref.dtype)

    @pl.when(ci == pl.num_programs(1) - 1)
    def _():
        for h in range(H_C):
            gi, e = divmod(h, RW_G)
            so_ref[0, h] = sbd_ref[gi, e * DH_C:(e + 1) * DH_C, e * DH_C:(e + 1) * DH_C]


def _rwkv_param_specs(const):
    row = lambda n: pl.BlockSpec((1, n), const)
    return [row(N_C), row(W_C), pl.BlockSpec((LANE, W_C), const), row(W_C), pl.BlockSpec((LANE, W_C), const),
            pl.BlockSpec((R_G, W_C), const), row(W_C), row(W_C), row(W_C), row(W_C), row(W_C)]


def _rwkv_prompt(proj, params, bsz, nc):
    ntp = bsz * nc * CHUNK
    col = lambda j: (lambda b, c: (b * nc + c, j))
    wide = OFF_C // W_C
    const = lambda b, c: (0, 0)
    lora_w = N_C - 3 * W_C
    return pl.pallas_call(
        _rwkv_prompt_body,
        grid=(bsz, nc),
        in_specs=[pl.BlockSpec((CHUNK, W_C), col(wide)), pl.BlockSpec((CHUNK, W_C), col(wide + 1)),
                  pl.BlockSpec((CHUNK, W_C), col(wide + 2)),
                  pl.BlockSpec((CHUNK, lora_w), col((OFF_C + 3 * W_C) // lora_w))] + _rwkv_param_specs(const),
        out_specs=[pl.BlockSpec((CHUNK, W_C), lambda b, c: (b * nc + c, 0)),
                   pl.BlockSpec((1, H_C, DH_C, DH_C), lambda b, c: (b, 0, 0, 0))],
        out_shape=[jax.ShapeDtypeStruct((ntp, W_C), _MXU_DT),
                   jax.ShapeDtypeStruct((bsz, H_C, DH_C, DH_C), F32)],
        scratch_shapes=[pltpu.VMEM((RW_NG, RW_W, RW_W), F32),
                        pltpu.VMEM((CHUNK, W_C), F32), pltpu.VMEM((CHUNK, W_C), F32),
                        pltpu.VMEM((CHUNK, W_C), F32), pltpu.VMEM((CHUNK, lora_w), F32)],
        compiler_params=_cparams("parallel", "arbitrary"),
        name="rwkv_prompt",
    )(proj, proj, proj, proj, *params)


def _rwkv_sample_body(r_ref, k_ref, v_ref, l_ref, sh_ref, mu_ref, w0_ref, w2_ref, a0_ref, a2_ref, g2_ref,
                      kk_ref, ka_ref, rk_ref, lnw_ref, lnb_ref, s_ref,
                      yc_ref, so_ref, row_ref, vt_ref, yt_ref):
    ones_blk = _block_ones(RW_W, DH_C)
    raws = [ref[...] for ref in (r_ref, k_ref, v_ref, l_ref)]
    sh = sh_ref[0]
    prevs = [sh[:, 0:W_C], sh[:, W_C:2 * W_C], sh[:, 2 * W_C:3 * W_C], sh[:, 3 * W_C:N_C]]
    prm = (mu_ref[...], w0_ref[...], w2_ref[...], a0_ref[...], a2_ref[...], g2_ref[...], kk_ref[...], ka_ref[...])
    r, lw, k2, v, a_vec, b_vec, g = _rwkv_pre(*raws, *prevs, prm, ones_blk)
    row_ref[0] = r
    row_ref[1] = jnp.exp(lw)
    row_ref[2] = k2
    row_ref[3] = a_vec
    row_ref[4] = b_vec
    vt_ref[...] = jnp.transpose(_pad_rows(v, LANE))
    yt_ref[...] = jnp.zeros_like(yt_ref)
    lane = lax.broadcasted_iota(jnp.int32, (DH_C, LANE), 1)

    def step(b, carry):
        rows = [row_ref[i, pl.ds(b, 1), :] for i in range(5)]
        pick = lane == b
        for h in range(H_C):
            sl = slice(h * DH_C, (h + 1) * DH_C)
            r_h, w_h, k_h, a_h, b_h = (x[:, sl] for x in rows)
            s = s_ref[0, b, h]
            v_col = jnp.sum(jnp.where(pick, vt_ref[sl, :], 0.0), axis=1, keepdims=True)
            sa = jnp.sum(s * a_h, axis=1, keepdims=True)
            s_new = s * w_h + sa * b_h + v_col * k_h
            so_ref[b, h] = s_new
            y_col = jnp.sum(s_new * r_h, axis=1, keepdims=True)
            yt_ref[sl, :] += jnp.where(pick, y_col, 0.0)
        return carry

    lax.fori_loop(0, SB, step, 0)
    y = jnp.transpose(yt_ref[...])[0:SB, :]
    out = _rwkv_post(y, r, k2, v, g, rk_ref[...], lnw_ref[...], lnb_ref[...], ones_blk)
    yc_ref[...] = out.astype(yc_ref.dtype)


def _rwkv_sample(proj, shift_st, params, s_st, layer, ntp, nb):
    r0 = ntp // SB
    col = lambda j: (lambda i: (r0 + i, j))
    wide = OFF_C // W_C
    const = lambda i: (0, 0)
    lora_w = N_C - 3 * W_C
    dec_b = nb * SB
    return pl.pallas_call(
        _rwkv_sample_body,
        grid=(nb,),
        in_specs=[pl.BlockSpec((SB, W_C), col(wide)), pl.BlockSpec((SB, W_C), col(wide + 1)),
                  pl.BlockSpec((SB, W_C), col(wide + 2)),
                  pl.BlockSpec((SB, lora_w), col((OFF_C + 3 * W_C) // lora_w)),
                  pl.BlockSpec((1, SB, N_C), lambda i: (layer, i, 0))] + _rwkv_param_specs(const)
                 + [pl.BlockSpec((1, SB, H_C, DH_C, DH_C), lambda i: (layer, i, 0, 0, 0))],
        out_specs=[pl.BlockSpec((SB, W_C), lambda i: (i, 0)),
                   pl.BlockSpec((SB, H_C, DH_C, DH_C), lambda i: (i, 0, 0, 0))],
        out_shape=[jax.ShapeDtypeStruct((dec_b, W_C), _MXU_DT),
                   jax.ShapeDtypeStruct((dec_b, H_C, DH_C, DH_C), F32)],
        scratch_shapes=[pltpu.VMEM((5, SB, W_C), F32), pltpu.VMEM((W_C, LANE), F32), pltpu.VMEM((W_C, LANE), F32)],
        compiler_params=_cparams("parallel"),
        name="rwkv_sample",
    )(proj, proj, proj, proj, shift_st, *params, s_st)


def _lane_row(pairs):
    row = jnp.zeros((LANE,), F32)
    for off, vec in pairs:
        row = lax.dynamic_update_slice(row, vec.astype(F32), (off,))
    return row[None]


def _prep_w_in(w):
    a0, b0, c0, g0 = 0, N_A, N_A + N_B, N_A + N_B + N_C
    small = jnp.concatenate([w[:, a0 + 4 * W_A:a0 + N_A], w[:, b0 + 4 * W_B:b0 + N_B]], axis=1)
    pad = jnp.zeros((w.shape[0], N_PROJ - OFF_S - small.shape[1]), w.dtype)
    out = jnp.concatenate([w[:, g0:g0 + N_GATE], w[:, a0:a0 + 4 * W_A], w[:, b0:b0 + 4 * W_B],
                           w[:, c0:c0 + N_C], small, pad], axis=1)
    return out.astype(_MXU_DT)


def kernel(x_prompt, x_sample, state_mlstm_c, state_mlstm_n, state_mlstm_m, state_gdn_s, state_gdn_conv,
           state_rwkv_s, state_rwkv_shift, meta_tokens, norm_mix_w, w_in, mlstm_b_i, mlstm_b_f, mlstm_norm_w,
           gdn_conv_w, gdn_a_log, gdn_dt_bias, gdn_norm_w, rwkv_mu, rwkv_w0, rwkv_w2, rwkv_a0, rwkv_a2,
           rwkv_g2, rwkv_k_k, rwkv_k_a, rwkv_r_k, rwkv_ln_w, rwkv_ln_b, w_branch_a, w_branch_b, w_branch_c,
           w_out, norm_mlp_w, w_up, w_down, final_norm_w):
    bsz, seq, _ = x_prompt.shape
    dec_b = x_sample.shape[0]
    depth = w_in.shape[0]
    assert x_sample.shape[1] == 1 and seq % CHUNK == 0 and dec_b % SB == 0
    t_pad = LEAD_PAD + N_META + seq
    nc = t_pad // CHUNK
    ntp = bsz * t_pad
    nb = dec_b // SB
    assert ntp % SB == 0

    lead = jnp.concatenate([jnp.zeros((LEAD_PAD, D_MODEL), F32), meta_tokens.astype(F32)], axis=0)
    xp = jnp.concatenate([jnp.broadcast_to(lead[None], (bsz, CHUNK, D_MODEL)), x_prompt], axis=1)
    x = jnp.concatenate([xp.reshape(ntp, D_MODEL), x_sample[:, 0]], axis=0)

    rows = jnp.arange(ntp + dec_b)
    keep = jnp.logical_or(rows >= ntp, (rows % t_pad) >= LEAD_PAD).astype(F32)[:, None]

    n_st = state_mlstm_n[:, :, :, None, :]
    m_st = jnp.swapaxes(state_mlstm_m, 1, 2)[..., None]
    conv_t = jnp.swapaxes(state_gdn_conv, 1, 2)
    row = lambda v: v.reshape(1, -1).astype(F32)

    outs = [[] for _ in range(14)]
    for l in range(depth):
        w_in_l = _prep_w_in(w_in[l])
        proj = _mm_rms(x, row(norm_mix_w[l]), w_in_l, F32, None, 1280, "mm_in")

        bias_row = _lane_row([(0, mlstm_b_i[l]), (H_A, mlstm_b_f[l])])
        nw_a = row(mlstm_norm_w[l])
        ya_p, pc_, pn_, pm_ = _mlstm_prompt(proj, bias_row, nw_a, bsz, nc)
        ya_s, sc_, sn_, sm_ = _mlstm_sample(proj, bias_row, nw_a, state_mlstm_c, n_st, m_st, l, ntp, nb)

        alog_row = _lane_row([(8, gdn_a_log[l])])
        dtb_row = _lane_row([(8, gdn_dt_bias[l])])
        nw_b = row(gdn_norm_w[l])
        conv_w = gdn_conv_w[l].astype(F32)
        yb_p, pgs_ = _gdn_prompt(proj, conv_w, alog_row, dtb_row, nw_b, bsz, nc)
        yb_s, sgs_, nq_, nk_, nv_ = _gdn_sample(proj, conv_t, conv_w, alog_row, dtb_row, nw_b, state_gdn_s,
                                                l, ntp, nb)
        proj_p = proj[:ntp].reshape(bsz, t_pad, N_PROJ)
        pgconv_ = proj_p[:, t_pad - (CONV_W - 1):, OFF_B:OFF_B + 3 * W_B]
        sgconv_ = jnp.swapaxes(jnp.concatenate([nq_, nk_, nv_], axis=-1), 0, 1)

        w2p = jnp.zeros((LANE, W_C), F32).at[0:R_W].set(rwkv_w2[l])
        a2p = jnp.zeros((LANE, W_C), F32).at[R_W:R_W + R_A].set(rwkv_a2[l])
        params = (row(rwkv_mu[l]), row(rwkv_w0[l]), w2p, row(rwkv_a0[l]), a2p, rwkv_g2[l].astype(F32),
                  row(rwkv_k_k[l]), row(rwkv_k_a[l]), row(rwkv_r_k[l]), row(rwkv_ln_w[l]), row(rwkv_ln_b[l]))
        yc_p, prs_ = _rwkv_prompt(proj, params, bsz, nc)
        yc_s, srs_ = _rwkv_sample(proj, state_rwkv_shift, params, state_rwkv_s, l, ntp, nb)
        prshift_ = proj_p[:, t_pad - 1, OFF_C:OFF_C + N_C]
        srshift_ = proj[ntp:, OFF_C:OFF_C + N_C]

        ya = jnp.concatenate([ya_p, ya_s], axis=0)
        yb = jnp.concatenate([yb_p, yb_s], axis=0)
        yc = jnp.concatenate([yc_p, yc_s], axis=0)
        merged = _mm_merge(ya, yb, yc, _cast(w_branch_a[l]), _cast(w_branch_b[l]), _cast(w_branch_c[l]), proj)
        x = _mm_res(merged, _cast(w_out[l]), x, keep, name="mm_out")
        u2 = _mm_rms(x, row(norm_mlp_w[l]), _cast(w_up[l]), _MXU_DT, "relu2", 1024, "mm_up")
        x = _mm_res(u2, _cast(w_down[l]), x, keep, name="mm_down")

        for acc, val in zip(outs, (pc_, pn_[:, :, 0], pm_[:, :, 0, 0], pgs_, pgconv_, prs_, prshift_,
                                   sc_, sn_[:, :, 0], jnp.swapaxes(sm_[..., 0], 0, 1), sgs_, sgconv_, srs_,
                                   srshift_)):
            acc.append(val)

    y = _final_rms(x, row(final_norm_w))
    y_prompt = y[:ntp].reshape(bsz, t_pad, D_MODEL)[:, CHUNK:]
    y_sample = y[ntp:][:, None]
    return (y_prompt, y_sample) + tuple(jnp.stack(acc) for acc in outs)
```

```python
import functools

import jax
import jax.numpy as jnp
from jax import lax
from jax.experimental import pallas as pl
from jax.experimental.pallas import tpu as pltpu

F32 = jnp.float32
_MXU_DT = jnp.bfloat16

D_MODEL = 2048
D_FF = 4 * D_MODEL
N_META = 16
CHUNK = 64
LEAD_PAD = CHUNK - N_META
H_A, DH_A, W_A = 4, 256, 1024
H_B, DH_B, W_B, CONV_W = 8, 128, 1024, 4
H_C, DH_C, W_C = 16, 64, 1024
R_W, R_A, R_G = 64, 64, 128
N_A = 4 * W_A + 2 * H_A
N_B = 4 * W_B + 2 * H_B
N_C = 3 * W_C + R_W + R_A + R_G
N_GATE = 3 * D_MODEL
RMS_EPS = 1e-6
GN_EPS = 64e-5
L2_EPS = 1e-12
NEG = -1e30

OFF_G = 0
OFF_A = OFF_G + N_GATE
OFF_B = OFF_A + 4 * W_A
OFF_C = OFF_B + 4 * W_B
OFF_S = OFF_C + N_C
N_PROJ = 17920
LANE = 128
TM_TARGET = 1100
VMEM_LIMIT = 56 * 1024 * 1024
SB = 16


def _cast(x):
    return x.astype(_MXU_DT)


def _mm(a, b):
    return jnp.dot(_cast(a), _cast(b), preferred_element_type=F32)


def _mm_nt(a, b):
    return lax.dot_general(_cast(a), _cast(b), (((1,), (1,)), ((), ())), preferred_element_type=F32)


def _mm_tn(a, b):
    return lax.dot_general(_cast(a), _cast(b), (((0,), (0,)), ((), ())), preferred_element_type=F32)


def _split(x):
    hi = x.astype(_MXU_DT)
    lo = (x - hi.astype(F32)).astype(_MXU_DT)
    return hi, lo


def _mm3(a, b):
    ah, al = _split(a)
    bh, bl = _split(b)
    return (jnp.dot(ah, bh, preferred_element_type=F32) + jnp.dot(ah, bl, preferred_element_type=F32)
            + jnp.dot(al, bh, preferred_element_type=F32))


def _mm2_lhs(a, b_exact):
    ah, al = _split(a)
    return jnp.dot(ah, b_exact, preferred_element_type=F32) + jnp.dot(al, b_exact, preferred_element_type=F32)


def _sigmoid(x):
    return 1.0 / (1.0 + jnp.exp(-x))


def _softplus(x):
    return jnp.maximum(x, 0.0) + jnp.log(1.0 + jnp.exp(-jnp.abs(x)))


def _log_sigmoid(x):
    return -_softplus(-x)


def _silu(x):
    return x * _sigmoid(x)


def _pick_tile(m, target, mult=16):
    best = None
    for t in range(mult, min(m, target) + 1, mult):
        if m % t == 0:
            best = t
    return m if best is None else best


def _cparams(*sem):
    return pltpu.CompilerParams(dimension_semantics=sem, vmem_limit_bytes=VMEM_LIMIT)


def _col_to_row(col):
    n = col.shape[0]
    return jnp.transpose(jnp.broadcast_to(col, (n, LANE)))[0:1, :]


def _pad_rows(x, n):
    if x.shape[0] == n:
        return x
    return jnp.concatenate([x, jnp.zeros((n - x.shape[0], x.shape[1]), x.dtype)], axis=0)


def _cumsum_rows(x):
    n = x.shape[0]
    row = lax.broadcasted_iota(jnp.int32, x.shape, 0)
    s = 1
    while s < n:
        x = x + jnp.where(row >= s, pltpu.roll(x, s, 0), 0.0)
        s *= 2
    return x


def _shift_rows(cur, prev, j):
    row = lax.broadcasted_iota(jnp.int32, cur.shape, 0)
    return jnp.where(row < j, pltpu.roll(prev, j, 0), pltpu.roll(cur, j, 0))


def _tri_inv(n_mats, eye, levels, mm):
    ts = [eye + n for n in n_mats]
    ps = list(n_mats)
    for _ in range(levels - 1):
        ps = [mm(p, p) for p in ps]
        ts = [t + mm(t, p) for t, p in zip(ts, ps)]
    return ts


def _mm_rms_body(x_ref, nw_ref, w_ref, o_ref, h_ref, *, act):
    @pl.when(pl.program_id(1) == 0)
    def _():
        x = x_ref[...]
        ms = jnp.mean(x * x, axis=-1, keepdims=True)
        h_ref[...] = (x * lax.rsqrt(ms + RMS_EPS) * nw_ref[...]).astype(h_ref.dtype)

    acc = jnp.dot(h_ref[...], w_ref[...], preferred_element_type=F32)
    if act == "relu2":
        acc = jnp.maximum(acc, 0.0)
        acc = acc * acc
    o_ref[...] = acc.astype(o_ref.dtype)


def _mm_rms(x, nw, w, out_dtype, act, tn, name):
    m, k = x.shape
    n = w.shape[1]
    tm = _pick_tile(m, TM_TARGET)
    return pl.pallas_call(
        functools.partial(_mm_rms_body, act=act),
        grid=(m // tm, n // tn),
        in_specs=[pl.BlockSpec((tm, k), lambda i, j: (i, 0)),
                  pl.BlockSpec((1, k), lambda i, j: (0, 0)),
                  pl.BlockSpec((k, tn), lambda i, j: (0, j))],
        out_specs=pl.BlockSpec((tm, tn), lambda i, j: (i, j)),
        out_shape=jax.ShapeDtypeStruct((m, n), out_dtype),
        scratch_shapes=[pltpu.VMEM((tm, k), _MXU_DT)],
        compiler_params=_cparams("parallel", "arbitrary"),
        name=name,
    )(x, nw, w)


def _mm_merge_body(ya_ref, yb_ref, yc_ref, wa_ref, wb_ref, wc_ref, g0_ref, g1_ref, g2_ref, o_ref):
    acc = _sigmoid(g0_ref[...]) * jnp.dot(ya_ref[...], wa_ref[...], preferred_element_type=F32)
    acc += _sigmoid(g1_ref[...]) * jnp.dot(yb_ref[...], wb_ref[...], preferred_element_type=F32)
    acc += _sigmoid(g2_ref[...]) * jnp.dot(yc_ref[...], wc_ref[...], preferred_element_type=F32)
    o_ref[...] = acc.astype(o_ref.dtype)


def _mm_merge(ya, yb, yc, wa, wb, wc, proj, tn=512):
    m, k = ya.shape
    n = wa.shape[1]
    tm = _pick_tile(m, TM_TARGET)
    nj = n // tn
    y_spec = pl.BlockSpec((tm, k), lambda i, j: (i, 0))
    w_spec = pl.BlockSpec((k, tn), lambda i, j: (0, j))
    g_specs = [pl.BlockSpec((tm, tn), functools.partial(lambda i, j, g: (i, g * nj + j), g=g)) for g in range(3)]
    return pl.pallas_call(
        _mm_merge_body,
        grid=(m // tm, nj),
        in_specs=[y_spec, y_spec, y_spec, w_spec, w_spec, w_spec] + g_specs,
        out_specs=pl.BlockSpec((tm, tn), lambda i, j: (i, j)),
        out_shape=jax.ShapeDtypeStruct((m, n), _MXU_DT),
        compiler_params=_cparams("parallel", "parallel"),
        name="mm_merge",
    )(ya, yb, yc, wa, wb, wc, proj, proj, proj)


def _mm_res_body(a_ref, w_ref, x_ref, keep_ref, o_ref):
    k = pl.program_id(2)

    @pl.when(k == 0)
    def _():
        o_ref[...] = x_ref[...]

    o_ref[...] += jnp.dot(a_ref[...], w_ref[...], preferred_element_type=F32)

    @pl.when(k == pl.num_programs(2) - 1)
    def _():
        o_ref[...] = jnp.where(keep_ref[...] > 0.0, o_ref[...], 0.0)


def _mm_res(a, w, x, keep, tn=1024, tk=2048, name="mm_res"):
    m, kk = a.shape
    n = w.shape[1]
    tm = _pick_tile(m, TM_TARGET)
    return pl.pallas_call(
        _mm_res_body,
        grid=(m // tm, n // tn, kk // tk),
        in_specs=[pl.BlockSpec((tm, tk), lambda i, j, k: (i, k)),
                  pl.BlockSpec((tk, tn), lambda i, j, k: (k, j)),
                  pl.BlockSpec((tm, tn), lambda i, j, k: (i, j)),
                  pl.BlockSpec((tm, 1), lambda i, j, k: (i, 0))],
        out_specs=pl.BlockSpec((tm, tn), lambda i, j, k: (i, j)),
        out_shape=jax.ShapeDtypeStruct((m, n), F32),
        compiler_params=_cparams("parallel", "parallel", "arbitrary"),
        name=name,
    )(a, w, x, keep)


def _rms_body(x_ref, w_ref, o_ref):
    x = x_ref[...]
    ms = jnp.mean(x * x, axis=-1, keepdims=True)
    o_ref[...] = x * lax.rsqrt(ms + RMS_EPS) * w_ref[...]


def _final_rms(x, w):
    m, d = x.shape
    tm = _pick_tile(m, TM_TARGET)
    return pl.pallas_call(
        _rms_body,
        grid=(m // tm,),
        in_specs=[pl.BlockSpec((tm, d), lambda i: (i, 0)), pl.BlockSpec((1, d), lambda i: (0, 0))],
        out_specs=pl.BlockSpec((tm, d), lambda i: (i, 0)),
        out_shape=jax.ShapeDtypeStruct((m, d), F32),
        compiler_params=_cparams("parallel"),
        name="final_rms",
    )(x, w)


def _mlstm_prompt_body(q_ref, k_ref, v_ref, o_ref, sm_ref, bias_ref, nw_ref,
                       ya_ref, c_ref, n_ref, m_ref):
    ci = pl.program_id(1)

    @pl.when(ci == 0)
    def _():
        c_ref[...] = jnp.zeros_like(c_ref)
        n_ref[...] = jnp.zeros_like(n_ref)
        m_ref[...] = jnp.zeros_like(m_ref)

    L = CHUNK
    row1 = lax.broadcasted_iota(jnp.int32, (L, 1), 0)
    is_pad = jnp.logical_and(ci == 0, row1 < LEAD_PAD)
    smb = sm_ref[...] + bias_ref[...]
    li_all = jnp.where(is_pad, NEG, smb)
    lf_all = jnp.where(is_pad, 0.0, _log_sigmoid(smb))
    li_t = jnp.transpose(_pad_rows(li_all, LANE))
    lf_t = jnp.transpose(_pad_rows(lf_all, LANE))
    rr = lax.broadcasted_iota(jnp.int32, (L, L), 0)
    cc = lax.broadcasted_iota(jnp.int32, (L, L), 1)
    tril = cc <= rr
    for h in range(H_A):
        sl = slice(h * DH_A, (h + 1) * DH_A)
        li_col = li_all[:, h:h + 1]
        lf_col = lf_all[:, H_A + h:H_A + h + 1]
        li_row = li_t[h:h + 1, :L]
        lf_row = lf_t[H_A + h:H_A + h + 1, :L]
        b_col = jnp.sum(jnp.where(tril, lf_row, 0.0), axis=1, keepdims=True)
        b_row = jnp.sum(jnp.where(rr <= cc, lf_col, 0.0), axis=0, keepdims=True)
        d = jnp.where(tril, b_col - b_row + li_row, NEG)
        m_prev = m_ref[0, h, 0:1, 0:1]
        inter = b_col + m_prev
        m_row = jnp.maximum(inter, jnp.max(d, axis=1, keepdims=True))
        w_intra = jnp.exp(d - m_row)
        w_inter = jnp.exp(inter - m_row)
        qh = q_ref[:, sl]
        kh = k_ref[:, sl] * (DH_A ** -0.5)
        vh = v_ref[:, sl]
        s = _mm_nt(qh, kh) * w_intra
        c_h = c_ref[0, h]
        n_h = n_ref[0, h]
        num = _mm(s, vh) + w_inter * _mm(qh, c_h)
        den = jnp.sum(s, axis=1, keepdims=True) + w_inter * jnp.sum(qh * n_h, axis=1, keepdims=True)
        den = jnp.maximum(jnp.abs(den), jnp.exp(-m_row))
        hh = num / den
        hh = hh * lax.rsqrt(jnp.mean(hh * hh, axis=-1, keepdims=True) + RMS_EPS) * nw_ref[:, sl]
        ya_ref[:, sl] = (_sigmoid(o_ref[:, sl]) * hh).astype(ya_ref.dtype)
        b_last = b_col[L - 1:L, :]
        g_col = b_last - b_col + li_col
        m_new = jnp.maximum(b_last + m_prev, jnp.max(g_col, axis=0, keepdims=True))
        kw = kh * jnp.exp(g_col - m_new)
        decay = jnp.exp(b_last + m_prev - m_new)
        c_ref[0, h] = decay * c_h + _mm_tn(kw, vh)
        n_ref[0, h] = decay * n_h + jnp.sum(kw, axis=0, keepdims=True)
        m_ref[0, h] = jnp.broadcast_to(m_new, (8, LANE))


def _mlstm_prompt(proj, bias_row, nw, bsz, nc):
    ntp = bsz * nc * CHUNK
    col = lambda j: (lambda b, c: (b * nc + c, j))
    wide = OFF_A // W_A
    return pl.pallas_call(
        _mlstm_prompt_body,
        grid=(bsz, nc),
        in_specs=[pl.BlockSpec((CHUNK, W_A), col(wide)), pl.BlockSpec((CHUNK, W_A), col(wide + 1)),
                  pl.BlockSpec((CHUNK, W_A), col(wide + 2)), pl.BlockSpec((CHUNK, W_A), col(wide + 3)),
                  pl.BlockSpec((CHUNK, LANE), col(OFF_S // LANE)),
                  pl.BlockSpec((1, LANE), lambda b, c: (0, 0)),
                  pl.BlockSpec((1, W_A), lambda b, c: (0, 0))],
        out_specs=[pl.BlockSpec((CHUNK, W_A), lambda b, c: (b * nc + c, 0)),
                   pl.BlockSpec((1, H_A, DH_A, DH_A), lambda b, c: (b, 0, 0, 0)),
                   pl.BlockSpec((1, H_A, 1, DH_A), lambda b, c: (b, 0, 0, 0)),
                   pl.BlockSpec((1, H_A, 8, LANE), lambda b, c: (b, 0, 0, 0))],
        out_shape=[jax.ShapeDtypeStruct((ntp, W_A), _MXU_DT),
                   jax.ShapeDtypeStruct((bsz, H_A, DH_A, DH_A), F32),
                   jax.ShapeDtypeStruct((bsz, H_A, 1, DH_A), F32),
                   jax.ShapeDtypeStruct((bsz, H_A, 8, LANE), F32)],
        compiler_params=_cparams("parallel", "arbitrary"),
        name="mlstm_prompt",
    )(proj, proj, proj, proj, proj, bias_row, nw)


def _lane_pick(x, idx):
    lane = lax.broadcasted_iota(jnp.int32, x.shape, 1)
    return jnp.sum(jnp.where(lane == idx, x, 0.0), axis=1, keepdims=True)


def _mlstm_sample_body(q_ref, k_ref, v_ref, o_ref, sm_ref, bias_ref, nw_ref, c_ref, n_ref, m_ref, *rest):
    ya_ref, co_ref, no_ref, mo_ref = rest[-4:]
    h = pl.program_id(1)
    smb = sm_ref[...] + bias_ref[...]
    li = _lane_pick(smb, h)
    lf = _lane_pick(_log_sigmoid(smb), H_A + h)
    m_prev = m_ref[0]
    inter = lf + m_prev
    m_new = jnp.maximum(inter, li)
    w_intra = jnp.exp(li - m_new)
    w_inter = jnp.exp(inter - m_new)
    q = q_ref[...]
    k = k_ref[...] * (DH_A ** -0.5)
    v = v_ref[...]
    s = jnp.sum(q * k, axis=1, keepdims=True) * w_intra
    q_t = jnp.transpose(_pad_rows(q, LANE))
    k_t = jnp.transpose(_pad_rows(k, LANE))
    mo_ref[0] = m_new
    nw = nw_ref[...]
    for b in range(SB):
        rb = slice(b, b + 1)
        c_b = c_ref[0, b, 0]
        n_b = n_ref[b, 0]
        q_col = q_t[:, rb]
        k_col = k_t[:, rb]
        qc = jnp.sum(q_col * c_b, axis=0, keepdims=True)
        qn = jnp.sum(q[rb] * n_b, axis=1, keepdims=True)
        num = s[rb] * v[rb] + w_inter[rb] * qc
        den = s[rb] + w_inter[rb] * qn
        den = jnp.maximum(jnp.abs(den), jnp.exp(-m_new[rb]))
        hh = num / den
        hh = hh * lax.rsqrt(jnp.mean(hh * hh, axis=-1, keepdims=True) + RMS_EPS) * nw
        ya_ref[rb, :] = (_sigmoid(o_ref[rb, :]) * hh).astype(ya_ref.dtype)
        co_ref[0, b, 0] = w_inter[rb] * c_b + (k_col * w_intra[rb]) * v[rb]
        no_ref[b, 0] = w_inter[rb] * n_b + w_intra[rb] * k[rb]


def _alias_args(prev_out, n_in, out_idx):
    if prev_out is None:
        return [], [], {}
    return [pl.BlockSpec(memory_space=pl.ANY)], [prev_out], {n_in: out_idx}


def _mlstm_sample(proj, bias_row, nw, c_st, n_st, m_st, c_prev, layer, ntp, nb):
    r0 = ntp // SB
    wide = OFF_A // DH_A
    col = lambda j: (lambda i, h: (r0 + i, j + h))
    dec_b = nb * SB
    in_specs = [pl.BlockSpec((SB, DH_A), col(wide)), pl.BlockSpec((SB, DH_A), col(wide + H_A)),
                pl.BlockSpec((SB, DH_A), col(wide + 2 * H_A)), pl.BlockSpec((SB, DH_A), col(wide + 3 * H_A)),
                pl.BlockSpec((SB, LANE), lambda i, h: (r0 + i, OFF_S // LANE)),
                pl.BlockSpec((1, LANE), lambda i, h: (0, 0)),
                pl.BlockSpec((1, DH_A), lambda i, h: (0, h)),
                pl.BlockSpec((1, SB, 1, DH_A, DH_A), lambda i, h: (layer, i, h, 0, 0)),
                pl.BlockSpec((None, SB, 1, 1, DH_A), lambda i, h: (layer, i, h, 0, 0)),
                pl.BlockSpec((None, 1, SB, 1), lambda i, h: (layer, h, i, 0))]
    a_specs, a_ops, aliases = _alias_args(c_prev, len(in_specs), 1)
    return pl.pallas_call(
        _mlstm_sample_body,
        grid=(nb, H_A),
        in_specs=in_specs + a_specs,
        out_specs=[pl.BlockSpec((SB, DH_A), lambda i, h: (i, h)),
                   pl.BlockSpec((1, SB, 1, DH_A, DH_A), lambda i, h: (layer, i, h, 0, 0)),
                   pl.BlockSpec((SB, 1, 1, DH_A), lambda i, h: (i, h, 0, 0)),
                   pl.BlockSpec((1, SB, 1), lambda i, h: (h, i, 0))],
        out_shape=[jax.ShapeDtypeStruct((dec_b, W_A), _MXU_DT),
                   jax.ShapeDtypeStruct(c_st.shape, F32),
                   jax.ShapeDtypeStruct((dec_b, H_A, 1, DH_A), F32),
                   jax.ShapeDtypeStruct((H_A, dec_b, 1), F32)],
        input_output_aliases=aliases,
        compiler_params=_cparams("parallel", "parallel"),
        name="mlstm_sample",
    )(proj, proj, proj, proj, proj, bias_row, nw, c_st, n_st, m_st, *a_ops)


GDN_G = 2
GDN_VT = GDN_G * CHUNK
GDN_W = GDN_G * DH_B
GDN_NG = H_B // GDN_G


def _gdn_gates(sm, alog_row, dtb_row):
    la = -jnp.exp(alog_row) * _softplus(sm + dtb_row)
    beta = _sigmoid(sm)
    return la, beta


def _gdn_post(o, z, nw):
    outs = []
    for h in range(H_B):
        sl = slice(h * DH_B, (h + 1) * DH_B)
        oh = o[:, sl]
        oh = oh * lax.rsqrt(jnp.mean(oh * oh, axis=-1, keepdims=True) + RMS_EPS) * nw
        outs.append(oh * _silu(z[:, sl]))
    return jnp.concatenate(outs, axis=1)


def _l2n_heads(x, scale):
    outs = []
    for h in range(H_B):
        xh = x[:, h * DH_B:(h + 1) * DH_B]
        outs.append(xh * (lax.rsqrt(jnp.sum(xh * xh, axis=-1, keepdims=True) + L2_EPS) * scale))
    return jnp.concatenate(outs, axis=1)


def _gdn_prompt_body(q_ref, k_ref, v_ref, z_ref, sm_ref, cw_ref, alog_ref, dtb_ref, nw_ref,
                     yb_ref, so_ref, sbd_ref, prev_ref):
    ci = pl.program_id(1)

    @pl.when(ci == 0)
    def _():
        sbd_ref[...] = jnp.zeros_like(sbd_ref)
        prev_ref[...] = jnp.zeros_like(prev_ref)

    L = CHUNK
    acts = []
    for p, ref in enumerate((q_ref, k_ref, v_ref)):
        u = ref[...]
        pu = prev_ref[p]
        cw = cw_ref[:, p * W_B:(p + 1) * W_B]
        y = cw[CONV_W - 1:CONV_W, :] * u
        for j in range(1, CONV_W):
            y = y + cw[CONV_W - 1 - j:CONV_W - j, :] * _shift_rows(u, pu, j)
        prev_ref[p] = u
        acts.append(_silu(y))
    qn = _l2n_heads(acts[0], DH_B ** -0.5)
    kn = _l2n_heads(acts[1], 1.0)
    va = acts[2]

    la_all, beta_all = _gdn_gates(sm_ref[...], alog_ref[...], dtb_ref[...])
    gc_all = _cumsum_rows(la_all)

    vt, w = GDN_VT, GDN_W
    ri = lax.broadcasted_iota(jnp.int32, (vt, vt), 0)
    cj = lax.broadcasted_iota(jnp.int32, (vt, vt), 1)
    same = (ri // L) == (cj // L)
    incl = jnp.logical_and(same, cj <= ri)
    strict = jnp.logical_and(same, cj < ri)
    eye = (ri == cj).astype(F32)
    bm = (lax.broadcasted_iota(jnp.int32, (vt, w), 0) // L) == (lax.broadcasted_iota(jnp.int32, (vt, w), 1) // DH_B)
    stack = lambda x: jnp.where(bm, jnp.concatenate([x] * GDN_G, axis=0), 0.0)

    grp = []
    for gi in range(GDN_NG):
        heads = [gi * GDN_G + e for e in range(GDN_G)]
        lsl = slice(gi * w, (gi + 1) * w)
        g_col = jnp.concatenate([gc_all[:, 8 + h:9 + h] for h in heads], axis=0)
        beta_col = jnp.concatenate([beta_all[:, 16 + h:17 + h] for h in heads], axis=0)
        glast_col = jnp.concatenate(
            [jnp.broadcast_to(gc_all[L - 1:L, 8 + h:9 + h], (L, 1)) for h in heads], axis=0)
        eg_state = jnp.concatenate(
            [jnp.broadcast_to(jnp.exp(gc_all[L - 1:L, 8 + h:9 + h]), (DH_B, 1)) for h in heads], axis=0)
        g_row = _col_to_row(g_col)
        k_s = stack(kn[:, lsl])
        q_s = stack(qn[:, lsl])
        v_s = stack(va[:, lsl])
        dec = jnp.exp(jnp.where(incl, g_col - g_row, NEG))
        a_mat = jnp.where(strict, beta_col * _mm_nt(k_s, k_s) * dec, 0.0)
        attn = jnp.where(incl, _mm_nt(q_s, k_s) * dec, 0.0)
        eg_col = jnp.exp(g_col)
        rhs = jnp.concatenate([v_s * beta_col, k_s * (beta_col * eg_col)], axis=1)
        grp.append((a_mat, attn, rhs, q_s * eg_col, k_s * jnp.exp(glast_col - g_col), eg_state))

    t_invs = _tri_inv([-g[0] for g in grp], eye, 6, _mm3)

    o_parts = []
    for gi in range(GDN_NG):
        _, attn, rhs, q_e, k_e, eg_state = grp[gi]
        sol = _mm(t_invs[gi], rhs)
        sbd = sbd_ref[gi]
        v_new = sol[:, :w] - _mm(sol[:, w:], sbd)
        o_m = _mm(q_e, sbd) + _mm(attn, v_new)
        o_g = o_m[0:L]
        for e in range(1, GDN_G):
            o_g = o_g + o_m[e * L:(e + 1) * L]
        o_parts.append(o_g)
        sbd_ref[gi] = eg_state * sbd + _mm_tn(k_e, v_new)

    o = jnp.concatenate(o_parts, axis=1)
    yb_ref[...] = _gdn_post(o, z_ref[...], nw_ref[...]).astype(yb_ref.dtype)

    @pl.when(ci == pl.num_programs(1) - 1)
    def _():
        for h in range(H_B):
            gi, e = divmod(h, GDN_G)
            so_ref[0, h] = sbd_ref[gi, e * DH_B:(e + 1) * DH_B, e * DH_B:(e + 1) * DH_B]


def _gdn_prompt(proj, conv_w, alog_row, dtb_row, nw, bsz, nc):
    ntp = bsz * nc * CHUNK
    col = lambda j: (lambda b, c: (b * nc + c, j))
    wide = OFF_B // W_B
    const = lambda b, c: (0, 0)
    return pl.pallas_call(
        _gdn_prompt_body,
        grid=(bsz, nc),
        in_specs=[pl.BlockSpec((CHUNK, W_B), col(wide)), pl.BlockSpec((CHUNK, W_B), col(wide + 1)),
                  pl.BlockSpec((CHUNK, W_B), col(wide + 2)), pl.BlockSpec((CHUNK, W_B), col(wide + 3)),
                  pl.BlockSpec((CHUNK, LANE), col(OFF_S // LANE)),
                  pl.BlockSpec((CONV_W, 3 * W_B), const),
                  pl.BlockSpec((1, LANE), const), pl.BlockSpec((1, LANE), const),
                  pl.BlockSpec((1, DH_B), const)],
        out_specs=[pl.BlockSpec((CHUNK, W_B), lambda b, c: (b * nc + c, 0)),
                   pl.BlockSpec((1, H_B, DH_B, DH_B), lambda b, c: (b, 0, 0, 0))],
        out_shape=[jax.ShapeDtypeStruct((ntp, W_B), _MXU_DT),
                   jax.ShapeDtypeStruct((bsz, H_B, DH_B, DH_B), F32)],
        scratch_shapes=[pltpu.VMEM((GDN_NG, GDN_W, GDN_W), F32),
                        pltpu.VMEM((3, CHUNK, W_B), F32)],
        compiler_params=_cparams("parallel", "arbitrary"),
        name="gdn_prompt",
    )(proj, proj, proj, proj, proj, conv_w, alog_row, dtb_row, nw)


def _gdn_sample_body(q_ref, k_ref, v_ref, z_ref, sm_ref, bq_ref, bk_ref, bv_ref, cwq_ref, cwk_ref, cwv_ref,
                     alog_ref, dtb_ref, nw_ref, s_ref, *rest):
    yb_ref, so_ref, nq_ref, nk_ref, nv_ref = rest[-5:]
    h = pl.program_id(1)
    acts = []
    for u_ref, b_ref, cw_ref, n_ref in ((q_ref, bq_ref, cwq_ref, nq_ref), (k_ref, bk_ref, cwk_ref, nk_ref),
                                        (v_ref, bv_ref, cwv_ref, nv_ref)):
        u = u_ref[...]
        y = cw_ref[CONV_W - 1:CONV_W, :] * u
        for j in range(CONV_W - 1):
            y = y + cw_ref[j:j + 1, :] * b_ref[j]
        for j in range(CONV_W - 2):
            n_ref[j] = b_ref[j + 1]
        n_ref[CONV_W - 2] = u
        acts.append(_silu(y))
    qa, ka, va = acts
    q = qa * (lax.rsqrt(jnp.sum(qa * qa, axis=-1, keepdims=True) + L2_EPS) * DH_B ** -0.5)
    k = ka * lax.rsqrt(jnp.sum(ka * ka, axis=-1, keepdims=True) + L2_EPS)
    la_all, beta_all = _gdn_gates(sm_ref[...], alog_ref[...], dtb_ref[...])
    eg = jnp.exp(_lane_pick(la_all, 8 + h))
    beta = _lane_pick(beta_all, 16 + h)
    qk = jnp.sum(q * k, axis=1, keepdims=True)
    q_t = jnp.transpose(_pad_rows(q, LANE))
    k_t = jnp.transpose(_pad_rows(k, LANE))
    nw = nw_ref[...]
    z = z_ref[...]
    for b in range(SB):
        rb = slice(b, b + 1)
        s_b = s_ref[0, b, 0]
        k_col = k_t[:, rb]
        q_col = q_t[:, rb]
        ks = jnp.sum(k_col * s_b, axis=0, keepdims=True)
        qs = jnp.sum(q_col * s_b, axis=0, keepdims=True)
        v_new = beta[rb] * va[rb] - (beta[rb] * eg[rb]) * ks
        o = eg[rb] * qs + qk[rb] * v_new
        o = o * lax.rsqrt(jnp.mean(o * o, axis=-1, keepdims=True) + RMS_EPS) * nw
        yb_ref[rb, :] = (o * _silu(z[rb])).astype(yb_ref.dtype)
        so_ref[0, b, 0] = eg[rb] * s_b + k_col * v_new


def _gdn_sample(proj, conv_t, conv_w, alog_row, dtb_row, nw, s_st, s_prev, layer, ntp, nb):
    r0 = ntp // SB
    base = OFF_B // DH_B
    col = lambda j: (lambda i, h: (r0 + i, base + j + h))
    const = lambda i, h: (0, 0)
    dec_b = nb * SB
    buf = lambda p: pl.BlockSpec((None, CONV_W - 1, SB, DH_B), lambda i, h: (layer, 0, i, p * H_B + h))
    cws = lambda p: pl.BlockSpec((CONV_W, DH_B), lambda i, h: (0, p * H_B + h))
    nbuf = pl.BlockSpec((CONV_W - 1, SB, DH_B), lambda i, h: (0, i, h))
    nshape = jax.ShapeDtypeStruct((CONV_W - 1, dec_b, W_B), F32)
    in_specs = [pl.BlockSpec((SB, DH_B), col(0)), pl.BlockSpec((SB, DH_B), col(H_B)),
                pl.BlockSpec((SB, DH_B), col(2 * H_B)), pl.BlockSpec((SB, DH_B), col(3 * H_B)),
                pl.BlockSpec((SB, LANE), lambda i, h: (r0 + i, OFF_S // LANE)),
                buf(0), buf(1), buf(2), cws(0), cws(1), cws(2),
                pl.BlockSpec((1, LANE), const), pl.BlockSpec((1, LANE), const), pl.BlockSpec((1, DH_B), const),
                pl.BlockSpec((1, SB, 1, DH_B, DH_B), lambda i, h: (layer, i, h, 0, 0))]
    a_specs, a_ops, aliases = _alias_args(s_prev, len(in_specs), 1)
    return pl.pallas_call(
        _gdn_sample_body,
        grid=(nb, H_B),
        in_specs=in_specs + a_specs,
        out_specs=[pl.BlockSpec((SB, DH_B), lambda i, h: (i, h)),
                   pl.BlockSpec((1, SB, 1, DH_B, DH_B), lambda i, h: (layer, i, h, 0, 0)),
                   nbuf, nbuf, nbuf],
        out_shape=[jax.ShapeDtypeStruct((dec_b, W_B), _MXU_DT),
                   jax.ShapeDtypeStruct(s_st.shape, F32),
                   nshape, nshape, nshape],
        input_output_aliases=aliases,
        compiler_params=_cparams("parallel", "parallel"),
        name="gdn_sample",
    )(proj, proj, proj, proj, proj, conv_t, conv_t, conv_t, conv_w, conv_w, conv_w,
      alog_row, dtb_row, nw, s_st, *a_ops)


RW_G = 4
RW_VT = RW_G * CHUNK
RW_W = RW_G * DH_C
RW_NG = H_C // RW_G


def _block_ones(n, seg):
    r = lax.broadcasted_iota(jnp.int32, (n, n), 0) // seg
    c = lax.broadcasted_iota(jnp.int32, (n, n), 1) // seg
    return (r == c).astype(_MXU_DT)


def _seg_sum(x, ones_blk):
    w = ones_blk.shape[0]
    parts = [_mm2_lhs(x[:, j * w:(j + 1) * w], ones_blk) for j in range(x.shape[1] // w)]
    return jnp.concatenate(parts, axis=1)


def _rwkv_pre(r_raw, k_raw, v_raw, l_raw, pr, pk, pv, pl_, prm, ones_blk):
    mu, w0, w2, a0, a2, g2, k_k, k_a = prm
    mix = lambda x, p, m: x + (p - x) * m
    r = mix(r_raw, pr, mu[:, 0:W_C])
    k = mix(k_raw, pk, mu[:, W_C:2 * W_C])
    v = mix(v_raw, pv, mu[:, 2 * W_C:3 * W_C])
    lo = mix(l_raw, pl_, mu[:, 3 * W_C:N_C])
    xw = lo[:, 0:R_W + R_A]
    xg = lo[:, R_W + R_A:R_W + R_A + R_G]
    lane = lax.broadcasted_iota(jnp.int32, xw.shape, 1)
    tw = jnp.where(lane < R_W, jnp.tanh(xw), 0.0)
    ta = jnp.where(lane < R_W, 0.0, xw)
    w_log = -_softplus(-(w0 + _mm(tw, w2))) - 0.5
    lw = -jnp.exp(w_log)
    a = _sigmoid(a0 + _mm(ta, a2))
    g = _mm(_sigmoid(xg), g2)
    kk = k * k_k
    kk = kk * lax.rsqrt(_seg_sum(kk * kk, ones_blk) + L2_EPS)
    k2 = k * (1.0 + (a - 1.0) * k_a)
    return r, lw, k2, v, -kk, kk * a, g


def _rwkv_post(y, r, k2, v, g, r_k, ln_w, ln_b, ones_blk):
    mean = _seg_sum(y, ones_blk) * (1.0 / DH_C)
    yc = y - mean
    var = _seg_sum(yc * yc, ones_blk) * (1.0 / DH_C)
    yn = yc * lax.rsqrt(var + GN_EPS) * ln_w + ln_b
    yn = yn + _seg_sum(r * k2 * r_k, ones_blk) * v
    return yn * g


def _rwkv_prompt_body(r_ref, k_ref, v_ref, l_ref, mu_ref, w0_ref, w2_ref, a0_ref, a2_ref, g2_ref,
                      kk_ref, ka_ref, rk_ref, lnw_ref, lnb_ref,
                      yc_ref, so_ref, sbd_ref, pr_ref, pk_ref, pv_ref, pl_ref):
    ci = pl.program_id(1)

    @pl.when(ci == 0)
    def _():
        sbd_ref[...] = jnp.zeros_like(sbd_ref)
        pr_ref[...] = jnp.zeros_like(pr_ref)
        pk_ref[...] = jnp.zeros_like(pk_ref)
        pv_ref[...] = jnp.zeros_like(pv_ref)
        pl_ref[...] = jnp.zeros_like(pl_ref)

    L = CHUNK
    ones_blk = _block_ones(RW_W, DH_C)
    raws = [ref[...] for ref in (r_ref, k_ref, v_ref, l_ref)]
    prevs = [_shift_rows(x, p[...], 1) for x, p in zip(raws, (pr_ref, pk_ref, pv_ref, pl_ref))]
    for x, p in zip(raws, (pr_ref, pk_ref, pv_ref, pl_ref)):
        p[...] = x
    prm = (mu_ref[...], w0_ref[...], w2_ref[...], a0_ref[...], a2_ref[...], g2_ref[...], kk_ref[...], ka_ref[...])
    r, lw, k2, v, a_vec, b_vec, g = _rwkv_pre(*raws, *prevs, prm, ones_blk)

    cl = _cumsum_rows(lw)
    e_incl = jnp.exp(cl)
    e_excl = jnp.exp(cl - lw)
    e_inv = jnp.exp(-cl)
    a_t = a_vec * e_excl
    r_t = r * e_incl
    b_t = b_vec * e_inv
    k_t = k2 * e_inv

    vt, w = RW_VT, RW_W
    ri = lax.broadcasted_iota(jnp.int32, (vt, vt), 0)
    cj = lax.broadcasted_iota(jnp.int32, (vt, vt), 1)
    same = (ri // L) == (cj // L)
    incl = jnp.logical_and(same, cj <= ri)
    strict = jnp.logical_and(same, cj < ri)
    eye = (ri == cj).astype(F32)
    bm = (lax.broadcasted_iota(jnp.int32, (vt, w), 0) // L) == (lax.broadcasted_iota(jnp.int32, (vt, w), 1) // DH_C)
    stack = lambda x: jnp.where(bm, jnp.concatenate([x] * RW_G, axis=0), 0.0)

    grp = []
    for gi in range(RW_NG):
        lsl = slice(gi * w, (gi + 1) * w)
        a_s, r_s, b_s, k_s, v_m = (stack(x[:, lsl]) for x in (a_t, r_t, b_t, k_t, v))
        sbd = sbd_ref[gi]
        ar = jnp.concatenate([a_s, r_s], axis=0)
        bk = jnp.concatenate([b_s, k_s], axis=0)
        gm = _mm_nt(ar, bk)
        ars = _mm_nt(ar, sbd)
        rhs_u = ars[:vt] + _mm(jnp.where(strict, gm[:vt, vt:], 0.0), v_m)
        rbk = jnp.concatenate([jnp.where(incl, gm[vt:, :vt], 0.0), jnp.where(incl, gm[vt:, vt:], 0.0)], axis=1)
        grp.append((jnp.where(strict, gm[:vt, :vt], 0.0), rhs_u, rbk, ars[vt:], v_m, bk, sbd))

    t_invs = _tri_inv([g[0] for g in grp], eye, 6, _mm)

    y_parts = []
    for gi in range(RW_NG):
        _, rhs_u, rbk, rs, v_m, bk, sbd = grp[gi]
        u_m = _mm(t_invs[gi], rhs_u)
        uv = jnp.concatenate([u_m, v_m], axis=0)
        y_m = rs + _mm(rbk, uv)
        y_g = y_m[0:L]
        for e in range(1, RW_G):
            y_g = y_g + y_m[e * L:(e + 1) * L]
        y_parts.append(y_g)
        sbd_ref[gi] = (sbd + _mm_tn(uv, bk)) * e_incl[L - 1:L, gi * w:(gi + 1) * w]

    y = jnp.concatenate(y_parts, axis=1)
    out = _rwkv_post(y, r, k2, v, g, rk_ref[...], lnw_ref[...], lnb_ref[...], ones_blk)
    yc_ref[...] = out.astype(yc_ref.dtype)

    @pl.when(ci == pl.num_programs(1) - 1)
    def _():
        for h in range(H_C):
            gi, e = divmod(h, RW_G)
            so_ref[0, h] = sbd_ref[gi, e * DH_C:(e + 1) * DH_C, e * DH_C:(e + 1) * DH_C]


def _rwkv_param_specs(const):
    row = lambda n: pl.BlockSpec((1, n), const)
    return [row(N_C), row(W_C), pl.BlockSpec((LANE, W_C), const), row(W_C), pl.BlockSpec((LANE, W_C), const),
            pl.BlockSpec((R_G, W_C), const), row(W_C), row(W_C), row(W_C), row(W_C), row(W_C)]


def _rwkv_prompt(proj, params, bsz, nc):
    ntp = bsz * nc * CHUNK
    col = lambda j: (lambda b, c: (b * nc + c, j))
    wide = OFF_C // W_C
    const = lambda b, c: (0, 0)
    lora_w = N_C - 3 * W_C
    return pl.pallas_call(
        _rwkv_prompt_body,
        grid=(bsz, nc),
        in_specs=[pl.BlockSpec((CHUNK, W_C), col(wide)), pl.BlockSpec((CHUNK, W_C), col(wide + 1)),
                  pl.BlockSpec((CHUNK, W_C), col(wide + 2)),
                  pl.BlockSpec((CHUNK, lora_w), col((OFF_C + 3 * W_C) // lora_w))] + _rwkv_param_specs(const),
        out_specs=[pl.BlockSpec((CHUNK, W_C), lambda b, c: (b * nc + c, 0)),
                   pl.BlockSpec((1, H_C, DH_C, DH_C), lambda b, c: (b, 0, 0, 0))],
        out_shape=[jax.ShapeDtypeStruct((ntp, W_C), _MXU_DT),
                   jax.ShapeDtypeStruct((bsz, H_C, DH_C, DH_C), F32)],
        scratch_shapes=[pltpu.VMEM((RW_NG, RW_W, RW_W), F32),
                        pltpu.VMEM((CHUNK, W_C), F32), pltpu.VMEM((CHUNK, W_C), F32),
                        pltpu.VMEM((CHUNK, W_C), F32), pltpu.VMEM((CHUNK, lora_w), F32)],
        compiler_params=_cparams("parallel", "arbitrary"),
        name="rwkv_prompt",
    )(proj, proj, proj, proj, *params)


def _rwkv_sample_body(r_ref, k_ref, v_ref, l_ref, sh_ref, mu_ref, w0_ref, w2_ref, a0_ref, a2_ref, g2_ref,
                      kk_ref, ka_ref, rk_ref, lnw_ref, lnb_ref, s_ref, *rest):
    yc_ref, so_ref, row_ref, vt_ref, yt_ref = rest[-5:]
    ones_blk = _block_ones(RW_W, DH_C)
    raws = [ref[...] for ref in (r_ref, k_ref, v_ref, l_ref)]
    sh = sh_ref[0]
    prevs = [sh[:, 0:W_C], sh[:, W_C:2 * W_C], sh[:, 2 * W_C:3 * W_C], sh[:, 3 * W_C:N_C]]
    prm = (mu_ref[...], w0_ref[...], w2_ref[...], a0_ref[...], a2_ref[...], g2_ref[...], kk_ref[...], ka_ref[...])
    r, lw, k2, v, a_vec, b_vec, g = _rwkv_pre(*raws, *prevs, prm, ones_blk)
    row_ref[0] = r
    row_ref[1] = jnp.exp(lw)
    row_ref[2] = k2
    row_ref[3] = a_vec
    row_ref[4] = b_vec
    vt_ref[...] = jnp.transpose(_pad_rows(v, LANE))
    yt_ref[...] = jnp.zeros_like(yt_ref)
    lane = lax.broadcasted_iota(jnp.int32, (DH_C, LANE), 1)

    def step(b, carry):
        rows = [row_ref[i, pl.ds(b, 1), :] for i in range(5)]
        pick = lane == b
        for h in range(H_C):
            sl = slice(h * DH_C, (h + 1) * DH_C)
            r_h, w_h, k_h, a_h, b_h = (x[:, sl] for x in rows)
            s = s_ref[0, b, h]
            v_col = jnp.sum(jnp.where(pick, vt_ref[sl, :], 0.0), axis=1, keepdims=True)
            sa = jnp.sum(s * a_h, axis=1, keepdims=True)
            s_new = s * w_h + sa * b_h + v_col * k_h
            so_ref[0, b, h] = s_new
            y_col = jnp.sum(s_new * r_h, axis=1, keepdims=True)
            yt_ref[sl, :] += jnp.where(pick, y_col, 0.0)
        return carry

    lax.fori_loop(0, SB, step, 0)
    y = jnp.transpose(yt_ref[...])[0:SB, :]
    out = _rwkv_post(y, r, k2, v, g, rk_ref[...], lnw_ref[...], lnb_ref[...], ones_blk)
    yc_ref[...] = out.astype(yc_ref.dtype)


def _rwkv_sample(proj, shift_st, params, s_st, s_prev, layer, ntp, nb):
    r0 = ntp // SB
    col = lambda j: (lambda i: (r0 + i, j))
    wide = OFF_C // W_C
    const = lambda i: (0, 0)
    lora_w = N_C - 3 * W_C
    dec_b = nb * SB
    in_specs = ([pl.BlockSpec((SB, W_C), col(wide)), pl.BlockSpec((SB, W_C), col(wide + 1)),
                 pl.BlockSpec((SB, W_C), col(wide + 2)),
                 pl.BlockSpec((SB, lora_w), col((OFF_C + 3 * W_C) // lora_w)),
                 pl.BlockSpec((1, SB, N_C), lambda i: (layer, i, 0))] + _rwkv_param_specs(const)
                + [pl.BlockSpec((1, SB, H_C, DH_C, DH_C), lambda i: (layer, i, 0, 0, 0))])
    a_specs, a_ops, aliases = _alias_args(s_prev, len(in_specs), 1)
    return pl.pallas_call(
        _rwkv_sample_body,
        grid=(nb,),
        in_specs=in_specs + a_specs,
        out_specs=[pl.BlockSpec((SB, W_C), lambda i: (i, 0)),
                   pl.BlockSpec((1, SB, H_C, DH_C, DH_C), lambda i: (layer, i, 0, 0, 0))],
        out_shape=[jax.ShapeDtypeStruct((dec_b, W_C), _MXU_DT),
                   jax.ShapeDtypeStruct(s_st.shape, F32)],
        scratch_shapes=[pltpu.VMEM((5, SB, W_C), F32), pltpu.VMEM((W_C, LANE), F32), pltpu.VMEM((W_C, LANE), F32)],
        input_output_aliases=aliases,
        compiler_params=_cparams("parallel"),
        name="rwkv_sample",
    )(proj, proj, proj, proj, shift_st, *params, s_st, *a_ops)


def _lane_row(pairs):
    row = jnp.zeros((LANE,), F32)
    for off, vec in pairs:
        row = lax.dynamic_update_slice(row, vec.astype(F32), (off,))
    return row[None]


def _prep_w_in(w):
    a0, b0, c0, g0 = 0, N_A, N_A + N_B, N_A + N_B + N_C
    small = jnp.concatenate([w[:, a0 + 4 * W_A:a0 + N_A], w[:, b0 + 4 * W_B:b0 + N_B]], axis=1)
    pad = jnp.zeros((w.shape[0], N_PROJ - OFF_S - small.shape[1]), w.dtype)
    out = jnp.concatenate([w[:, g0:g0 + N_GATE], w[:, a0:a0 + 4 * W_A], w[:, b0:b0 + 4 * W_B],
                           w[:, c0:c0 + N_C], small, pad], axis=1)
    return out.astype(_MXU_DT)


def kernel(x_prompt, x_sample, state_mlstm_c, state_mlstm_n, state_mlstm_m, state_gdn_s, state_gdn_conv,
           state_rwkv_s, state_rwkv_shift, meta_tokens, norm_mix_w, w_in, mlstm_b_i, mlstm_b_f, mlstm_norm_w,
           gdn_conv_w, gdn_a_log, gdn_dt_bias, gdn_norm_w, rwkv_mu, rwkv_w0, rwkv_w2, rwkv_a0, rwkv_a2,
           rwkv_g2, rwkv_k_k, rwkv_k_a, rwkv_r_k, rwkv_ln_w, rwkv_ln_b, w_branch_a, w_branch_b, w_branch_c,
           w_out, norm_mlp_w, w_up, w_down, final_norm_w):
    bsz, seq, _ = x_prompt.shape
    dec_b = x_sample.shape[0]
    depth = w_in.shape[0]
    assert x_sample.shape[1] == 1 and seq % CHUNK == 0 and dec_b % SB == 0
    t_pad = LEAD_PAD + N_META + seq
    nc = t_pad // CHUNK
    ntp = bsz * t_pad
    nb = dec_b // SB
    assert ntp % SB == 0

    lead = jnp.concatenate([jnp.zeros((LEAD_PAD, D_MODEL), F32), meta_tokens.astype(F32)], axis=0)
    xp = jnp.concatenate([jnp.broadcast_to(lead[None], (bsz, CHUNK, D_MODEL)), x_prompt], axis=1)
    x = jnp.concatenate([xp.reshape(ntp, D_MODEL), x_sample[:, 0]], axis=0)

    rows = jnp.arange(ntp + dec_b)
    keep = jnp.logical_or(rows >= ntp, (rows % t_pad) >= LEAD_PAD).astype(F32)[:, None]

    n_st = state_mlstm_n[:, :, :, None, :]
    m_st = jnp.swapaxes(state_mlstm_m, 1, 2)[..., None]
    conv_t = jnp.swapaxes(state_gdn_conv, 1, 2)
    row = lambda v: v.reshape(1, -1).astype(F32)

    outs = [[] for _ in range(14)]
    s_c = s_gs = s_rs = None
    for l in range(depth):
        w_in_l = _prep_w_in(w_in[l])
        proj = _mm_rms(x, row(norm_mix_w[l]), w_in_l, F32, None, 1280, "mm_in")
        last_rows = lambda n, c0, c1: jnp.stack(
            [proj[(b + 1) * t_pad - n:(b + 1) * t_pad, c0:c1] for b in range(bsz)])

        bias_row = _lane_row([(0, mlstm_b_i[l]), (H_A, mlstm_b_f[l])])
        nw_a = row(mlstm_norm_w[l])
        ya_p, pc_, pn_, pm_ = _mlstm_prompt(proj, bias_row, nw_a, bsz, nc)
        ya_s, s_c, sn_, sm_ = _mlstm_sample(proj, bias_row, nw_a, state_mlstm_c, n_st, m_st, s_c, l, ntp, nb)

        alog_row = _lane_row([(8, gdn_a_log[l])])
        dtb_row = _lane_row([(8, gdn_dt_bias[l])])
        nw_b = row(gdn_norm_w[l])
        conv_w = gdn_conv_w[l].astype(F32)
        yb_p, pgs_ = _gdn_prompt(proj, conv_w, alog_row, dtb_row, nw_b, bsz, nc)
        yb_s, s_gs, nq_, nk_, nv_ = _gdn_sample(proj, conv_t, conv_w, alog_row, dtb_row, nw_b, state_gdn_s,
                                                s_gs, l, ntp, nb)
        pgconv_ = last_rows(CONV_W - 1, OFF_B, OFF_B + 3 * W_B)
        sgconv_ = jnp.swapaxes(jnp.concatenate([nq_, nk_, nv_], axis=-1), 0, 1)

        w2p = jnp.zeros((LANE, W_C), F32).at[0:R_W].set(rwkv_w2[l])
        a2p = jnp.zeros((LANE, W_C), F32).at[R_W:R_W + R_A].set(rwkv_a2[l])
        params = (row(rwkv_mu[l]), row(rwkv_w0[l]), w2p, row(rwkv_a0[l]), a2p, rwkv_g2[l].astype(F32),
                  row(rwkv_k_k[l]), row(rwkv_k_a[l]), row(rwkv_r_k[l]), row(rwkv_ln_w[l]), row(rwkv_ln_b[l]))
        yc_p, prs_ = _rwkv_prompt(proj, params, bsz, nc)
        yc_s, s_rs = _rwkv_sample(proj, state_rwkv_shift, params, state_rwkv_s, s_rs, l, ntp, nb)
        prshift_ = last_rows(1, OFF_C, OFF_C + N_C)[:, 0]
        srshift_ = proj[ntp:, OFF_C:OFF_C + N_C]

        ya = jnp.concatenate([ya_p, ya_s], axis=0)
        yb = jnp.concatenate([yb_p, yb_s], axis=0)
        yc = jnp.concatenate([yc_p, yc_s], axis=0)
        merged = _mm_merge(ya, yb, yc, _cast(w_branch_a[l]), _cast(w_branch_b[l]), _cast(w_branch_c[l]), proj)
        x = _mm_res(merged, _cast(w_out[l]), x, keep, name="mm_out")
        u2 = _mm_rms(x, row(norm_mlp_w[l]), _cast(w_up[l]), _MXU_DT, "relu2", 1024, "mm_up")
        x = _mm_res(u2, _cast(w_down[l]), x, keep, name="mm_down")

        for acc, val in zip(outs, (pc_, pn_[:, :, 0], pm_[:, :, 0, 0], pgs_, pgconv_, prs_, prshift_,
                                   None, sn_[:, :, 0], jnp.swapaxes(sm_[..., 0], 0, 1), None, sgconv_, None,
                                   srshift_)):
            acc.append(val)

    y = _final_rms(x, row(final_norm_w))
    y_prompt = y[:ntp].reshape(bsz, t_pad, D_MODEL)[:, CHUNK:]
    y_sample = y[ntp:][:, None]
    stacked = {7: s_c, 10: s_gs, 12: s_rs}
    return (y_prompt, y_sample) + tuple(stacked[i] if i in stacked else jnp.stack(acc)
                                        for i, acc in enumerate(outs))
```

```python
import functools

import jax
import jax.numpy as jnp
from jax import lax
from jax.experimental import pallas as pl
from jax.experimental.pallas import tpu as pltpu

F32 = jnp.float32
_MXU_DT = jnp.bfloat16

D_MODEL = 2048
D_FF = 4 * D_MODEL
N_META = 16
CHUNK = 64
LEAD_PAD = CHUNK - N_META
H_A, DH_A, W_A = 4, 256, 1024
H_B, DH_B, W_B, CONV_W = 8, 128, 1024, 4
H_C, DH_C, W_C = 16, 64, 1024
R_W, R_A, R_G = 64, 64, 128
N_A = 4 * W_A + 2 * H_A
N_B = 4 * W_B + 2 * H_B
N_C = 3 * W_C + R_W + R_A + R_G
N_GATE = 3 * D_MODEL
RMS_EPS = 1e-6
GN_EPS = 64e-5
L2_EPS = 1e-12
NEG = -1e30

OFF_G = 0
OFF_A = OFF_G + N_GATE
OFF_B = OFF_A + 4 * W_A
OFF_C = OFF_B + 4 * W_B
OFF_S = OFF_C + N_C
N_PROJ = 17920
LANE = 128
TM_TARGET = 1100
VMEM_LIMIT = 56 * 1024 * 1024
SB = 16


def _cast(x):
    return x.astype(_MXU_DT)


def _mm(a, b):
    return jnp.dot(_cast(a), _cast(b), preferred_element_type=F32)


def _mm_nt(a, b):
    return lax.dot_general(_cast(a), _cast(b), (((1,), (1,)), ((), ())), preferred_element_type=F32)


def _mm_tn(a, b):
    return lax.dot_general(_cast(a), _cast(b), (((0,), (0,)), ((), ())), preferred_element_type=F32)


def _split(x):
    hi = x.astype(_MXU_DT)
    lo = (x - hi.astype(F32)).astype(_MXU_DT)
    return hi, lo


def _mm3(a, b):
    ah, al = _split(a)
    bh, bl = _split(b)
    return (jnp.dot(ah, bh, preferred_element_type=F32) + jnp.dot(ah, bl, preferred_element_type=F32)
            + jnp.dot(al, bh, preferred_element_type=F32))


def _mm2_lhs(a, b_exact):
    ah, al = _split(a)
    return jnp.dot(ah, b_exact, preferred_element_type=F32) + jnp.dot(al, b_exact, preferred_element_type=F32)


def _sigmoid(x):
    return 1.0 / (1.0 + jnp.exp(-x))


def _softplus(x):
    return jnp.maximum(x, 0.0) + jnp.log(1.0 + jnp.exp(-jnp.abs(x)))


def _log_sigmoid(x):
    return -_softplus(-x)


def _silu(x):
    return x * _sigmoid(x)


def _pick_tile(m, target, mult=16):
    best = None
    for t in range(mult, min(m, target) + 1, mult):
        if m % t == 0:
            best = t
    return m if best is None else best


def _cparams(*sem):
    return pltpu.CompilerParams(dimension_semantics=sem, vmem_limit_bytes=VMEM_LIMIT)


def _col_to_row(col):
    n = col.shape[0]
    return jnp.transpose(jnp.broadcast_to(col, (n, LANE)))[0:1, :]


def _pad_rows(x, n):
    if x.shape[0] == n:
        return x
    return jnp.concatenate([x, jnp.zeros((n - x.shape[0], x.shape[1]), x.dtype)], axis=0)


def _cumsum_rows(x):
    n = x.shape[0]
    row = lax.broadcasted_iota(jnp.int32, x.shape, 0)
    s = 1
    while s < n:
        x = x + jnp.where(row >= s, pltpu.roll(x, s, 0), 0.0)
        s *= 2
    return x


def _shift_rows(cur, prev, j):
    row = lax.broadcasted_iota(jnp.int32, cur.shape, 0)
    return jnp.where(row < j, pltpu.roll(prev, j, 0), pltpu.roll(cur, j, 0))


def _tri_inv(n_mats, eye, levels, mm):
    ts = [eye + n for n in n_mats]
    ps = list(n_mats)
    for _ in range(levels - 1):
        ps = [mm(p, p) for p in ps]
        ts = [t + mm(t, p) for t, p in zip(ts, ps)]
    return ts


def _mm_rms_body(x_ref, nw_ref, w_ref, o_ref, h_ref, *, act):
    @pl.when(pl.program_id(1) == 0)
    def _():
        x = x_ref[...]
        ms = jnp.mean(x * x, axis=-1, keepdims=True)
        h_ref[...] = (x * lax.rsqrt(ms + RMS_EPS) * nw_ref[...]).astype(h_ref.dtype)

    acc = jnp.dot(h_ref[...], w_ref[...], preferred_element_type=F32)
    if act == "relu2":
        acc = jnp.maximum(acc, 0.0)
        acc = acc * acc
    o_ref[...] = acc.astype(o_ref.dtype)


def _mm_rms(x, nw, w, layer, out_dtype, act, tn, name):
    m, k = x.shape
    n = w.shape[2]
    tm = _pick_tile(m, TM_TARGET)
    return pl.pallas_call(
        functools.partial(_mm_rms_body, act=act),
        grid=(m // tm, n // tn),
        in_specs=[pl.BlockSpec((tm, k), lambda i, j: (i, 0)),
                  pl.BlockSpec((1, k), lambda i, j: (0, 0)),
                  pl.BlockSpec((None, k, tn), lambda i, j: (layer, 0, j))],
        out_specs=pl.BlockSpec((tm, tn), lambda i, j: (i, j)),
        out_shape=jax.ShapeDtypeStruct((m, n), out_dtype),
        scratch_shapes=[pltpu.VMEM((tm, k), _MXU_DT)],
        compiler_params=_cparams("parallel", "arbitrary"),
        name=name,
    )(x, nw, w)


def _mm_merge_body(ya_ref, yb_ref, yc_ref, wa_ref, wb_ref, wc_ref, g0_ref, g1_ref, g2_ref, o_ref):
    acc = _sigmoid(g0_ref[...]) * jnp.dot(ya_ref[...], wa_ref[...], preferred_element_type=F32)
    acc += _sigmoid(g1_ref[...]) * jnp.dot(yb_ref[...], wb_ref[...], preferred_element_type=F32)
    acc += _sigmoid(g2_ref[...]) * jnp.dot(yc_ref[...], wc_ref[...], preferred_element_type=F32)
    o_ref[...] = acc.astype(o_ref.dtype)


def _mm_merge(ya, yb, yc, wa, wb, wc, layer, proj, tn=512):
    m, k = ya.shape
    n = wa.shape[2]
    tm = _pick_tile(m, TM_TARGET)
    nj = n // tn
    y_spec = pl.BlockSpec((tm, k), lambda i, j: (i, 0))
    w_spec = pl.BlockSpec((None, k, tn), lambda i, j: (layer, 0, j))
    g_specs = [pl.BlockSpec((tm, tn), functools.partial(lambda i, j, g: (i, g * nj + j), g=g)) for g in range(3)]
    return pl.pallas_call(
        _mm_merge_body,
        grid=(m // tm, nj),
        in_specs=[y_spec, y_spec, y_spec, w_spec, w_spec, w_spec] + g_specs,
        out_specs=pl.BlockSpec((tm, tn), lambda i, j: (i, j)),
        out_shape=jax.ShapeDtypeStruct((m, n), _MXU_DT),
        compiler_params=_cparams("parallel", "parallel"),
        name="mm_merge",
    )(ya, yb, yc, wa, wb, wc, proj, proj, proj)


def _mm_res_body(a_ref, w_ref, x_ref, keep_ref, o_ref):
    k = pl.program_id(2)

    @pl.when(k == 0)
    def _():
        o_ref[...] = x_ref[...]

    o_ref[...] += jnp.dot(a_ref[...], w_ref[...], preferred_element_type=F32)

    @pl.when(k == pl.num_programs(2) - 1)
    def _():
        o_ref[...] = jnp.where(keep_ref[...] > 0.0, o_ref[...], 0.0)


def _mm_res(a, w, layer, x, keep, tn=1024, tk=2048, name="mm_res"):
    m, kk = a.shape
    n = w.shape[2]
    tm = _pick_tile(m, TM_TARGET)
    return pl.pallas_call(
        _mm_res_body,
        grid=(m // tm, n // tn, kk // tk),
        in_specs=[pl.BlockSpec((tm, tk), lambda i, j, k: (i, k)),
                  pl.BlockSpec((None, tk, tn), lambda i, j, k: (layer, k, j)),
                  pl.BlockSpec((tm, tn), lambda i, j, k: (i, j)),
                  pl.BlockSpec((tm, 1), lambda i, j, k: (i, 0))],
        out_specs=pl.BlockSpec((tm, tn), lambda i, j, k: (i, j)),
        out_shape=jax.ShapeDtypeStruct((m, n), F32),
        compiler_params=_cparams("parallel", "parallel", "arbitrary"),
        name=name,
    )(a, w, x, keep)


def _rms_body(x_ref, w_ref, o_ref):
    x = x_ref[...]
    ms = jnp.mean(x * x, axis=-1, keepdims=True)
    o_ref[...] = x * lax.rsqrt(ms + RMS_EPS) * w_ref[...]


def _final_rms(x, w):
    m, d = x.shape
    tm = _pick_tile(m, TM_TARGET)
    return pl.pallas_call(
        _rms_body,
        grid=(m // tm,),
        in_specs=[pl.BlockSpec((tm, d), lambda i: (i, 0)), pl.BlockSpec((1, d), lambda i: (0, 0))],
        out_specs=pl.BlockSpec((tm, d), lambda i: (i, 0)),
        out_shape=jax.ShapeDtypeStruct((m, d), F32),
        compiler_params=_cparams("parallel"),
        name="final_rms",
    )(x, w)


def _mlstm_prompt_body(q_ref, k_ref, v_ref, o_ref, sm_ref, bias_ref, nw_ref,
                       ya_ref, c_ref, n_ref, m_ref):
    ci = pl.program_id(1)

    @pl.when(ci == 0)
    def _():
        c_ref[...] = jnp.zeros_like(c_ref)
        n_ref[...] = jnp.zeros_like(n_ref)
        m_ref[...] = jnp.zeros_like(m_ref)

    L = CHUNK
    row1 = lax.broadcasted_iota(jnp.int32, (L, 1), 0)
    is_pad = jnp.logical_and(ci == 0, row1 < LEAD_PAD)
    smb = sm_ref[...] + bias_ref[...]
    li_all = jnp.where(is_pad, NEG, smb)
    lf_all = jnp.where(is_pad, 0.0, _log_sigmoid(smb))
    li_t = jnp.transpose(_pad_rows(li_all, LANE))
    lf_t = jnp.transpose(_pad_rows(lf_all, LANE))
    rr = lax.broadcasted_iota(jnp.int32, (L, L), 0)
    cc = lax.broadcasted_iota(jnp.int32, (L, L), 1)
    tril = cc <= rr
    for h in range(H_A):
        sl = slice(h * DH_A, (h + 1) * DH_A)
        li_col = li_all[:, h:h + 1]
        lf_col = lf_all[:, H_A + h:H_A + h + 1]
        li_row = li_t[h:h + 1, :L]
        lf_row = lf_t[H_A + h:H_A + h + 1, :L]
        b_col = jnp.sum(jnp.where(tril, lf_row, 0.0), axis=1, keepdims=True)
        b_row = jnp.sum(jnp.where(rr <= cc, lf_col, 0.0), axis=0, keepdims=True)
        d = jnp.where(tril, b_col - b_row + li_row, NEG)
        m_prev = m_ref[0, h, 0:1, 0:1]
        inter = b_col + m_prev
        m_row = jnp.maximum(inter, jnp.max(d, axis=1, keepdims=True))
        w_intra = jnp.exp(d - m_row)
        w_inter = jnp.exp(inter - m_row)
        qh = q_ref[:, sl]
        kh = k_ref[:, sl] * (DH_A ** -0.5)
        vh = v_ref[:, sl]
        s = _mm_nt(qh, kh) * w_intra
        c_h = c_ref[0, h]
        n_h = n_ref[0, h]
        num = _mm(s, vh) + w_inter * _mm(qh, c_h)
        den = jnp.sum(s, axis=1, keepdims=True) + w_inter * jnp.sum(qh * n_h, axis=1, keepdims=True)
        den = jnp.maximum(jnp.abs(den), jnp.exp(-m_row))
        hh = num / den
        hh = hh * lax.rsqrt(jnp.mean(hh * hh, axis=-1, keepdims=True) + RMS_EPS) * nw_ref[:, sl]
        ya_ref[:, sl] = (_sigmoid(o_ref[:, sl]) * hh).astype(ya_ref.dtype)
        b_last = b_col[L - 1:L, :]
        g_col = b_last - b_col + li_col
        m_new = jnp.maximum(b_last + m_prev, jnp.max(g_col, axis=0, keepdims=True))
        kw = kh * jnp.exp(g_col - m_new)
        decay = jnp.exp(b_last + m_prev - m_new)
        c_ref[0, h] = decay * c_h + _mm_tn(kw, vh)
        n_ref[0, h] = decay * n_h + jnp.sum(kw, axis=0, keepdims=True)
        m_ref[0, h] = jnp.broadcast_to(m_new, (8, LANE))


def _mlstm_prompt(proj, bias_row, nw, bsz, nc):
    ntp = bsz * nc * CHUNK
    col = lambda j: (lambda b, c: (b * nc + c, j))
    wide = OFF_A // W_A
    return pl.pallas_call(
        _mlstm_prompt_body,
        grid=(bsz, nc),
        in_specs=[pl.BlockSpec((CHUNK, W_A), col(wide)), pl.BlockSpec((CHUNK, W_A), col(wide + 1)),
                  pl.BlockSpec((CHUNK, W_A), col(wide + 2)), pl.BlockSpec((CHUNK, W_A), col(wide + 3)),
                  pl.BlockSpec((CHUNK, LANE), col(OFF_S // LANE)),
                  pl.BlockSpec((1, LANE), lambda b, c: (0, 0)),
                  pl.BlockSpec((1, W_A), lambda b, c: (0, 0))],
        out_specs=[pl.BlockSpec((CHUNK, W_A), lambda b, c: (b * nc + c, 0)),
                   pl.BlockSpec((1, H_A, DH_A, DH_A), lambda b, c: (b, 0, 0, 0)),
                   pl.BlockSpec((1, H_A, 1, DH_A), lambda b, c: (b, 0, 0, 0)),
                   pl.BlockSpec((1, H_A, 8, LANE), lambda b, c: (b, 0, 0, 0))],
        out_shape=[jax.ShapeDtypeStruct((ntp, W_A), _MXU_DT),
                   jax.ShapeDtypeStruct((bsz, H_A, DH_A, DH_A), F32),
                   jax.ShapeDtypeStruct((bsz, H_A, 1, DH_A), F32),
                   jax.ShapeDtypeStruct((bsz, H_A, 8, LANE), F32)],
        compiler_params=_cparams("parallel", "arbitrary"),
        name="mlstm_prompt",
    )(proj, proj, proj, proj, proj, bias_row, nw)


def _lane_pick(x, idx):
    lane = lax.broadcasted_iota(jnp.int32, x.shape, 1)
    return jnp.sum(jnp.where(lane == idx, x, 0.0), axis=1, keepdims=True)


def _mlstm_sample_body(q_ref, k_ref, v_ref, o_ref, sm_ref, bias_ref, nw_ref, c_ref, n_ref, m_ref, *rest):
    ya_ref, co_ref, no_ref, mo_ref = rest[-4:]
    h = pl.program_id(1)
    smb = sm_ref[...] + bias_ref[...]
    li = _lane_pick(smb, h)
    lf = _lane_pick(_log_sigmoid(smb), H_A + h)
    m_prev = m_ref[0]
    inter = lf + m_prev
    m_new = jnp.maximum(inter, li)
    w_intra = jnp.exp(li - m_new)
    w_inter = jnp.exp(inter - m_new)
    q = q_ref[...]
    k = k_ref[...] * (DH_A ** -0.5)
    v = v_ref[...]
    s = jnp.sum(q * k, axis=1, keepdims=True) * w_intra
    q_t = jnp.transpose(_pad_rows(q, LANE))
    k_t = jnp.transpose(_pad_rows(k, LANE))
    mo_ref[0] = m_new
    nw = nw_ref[...]
    for b in range(SB):
        rb = slice(b, b + 1)
        c_b = c_ref[0, b, 0]
        n_b = n_ref[b, 0]
        q_col = q_t[:, rb]
        k_col = k_t[:, rb]
        qc = jnp.sum(q_col * c_b, axis=0, keepdims=True)
        qn = jnp.sum(q[rb] * n_b, axis=1, keepdims=True)
        num = s[rb] * v[rb] + w_inter[rb] * qc
        den = s[rb] + w_inter[rb] * qn
        den = jnp.maximum(jnp.abs(den), jnp.exp(-m_new[rb]))
        hh = num / den
        hh = hh * lax.rsqrt(jnp.mean(hh * hh, axis=-1, keepdims=True) + RMS_EPS) * nw
        ya_ref[rb, :] = (_sigmoid(o_ref[rb, :]) * hh).astype(ya_ref.dtype)
        co_ref[0, b, 0] = w_inter[rb] * c_b + (k_col * w_intra[rb]) * v[rb]
        no_ref[b, 0] = w_inter[rb] * n_b + w_intra[rb] * k[rb]


def _alias_args(prev_out, n_in, out_idx):
    if prev_out is None:
        return [], [], {}
    return [pl.BlockSpec(memory_space=pl.ANY)], [prev_out], {n_in: out_idx}


def _mlstm_sample(proj, bias_row, nw, c_st, n_st, m_st, c_prev, layer, ntp, nb):
    r0 = ntp // SB
    wide = OFF_A // DH_A
    col = lambda j: (lambda i, h: (r0 + i, j + h))
    dec_b = nb * SB
    in_specs = [pl.BlockSpec((SB, DH_A), col(wide)), pl.BlockSpec((SB, DH_A), col(wide + H_A)),
                pl.BlockSpec((SB, DH_A), col(wide + 2 * H_A)), pl.BlockSpec((SB, DH_A), col(wide + 3 * H_A)),
                pl.BlockSpec((SB, LANE), lambda i, h: (r0 + i, OFF_S // LANE)),
                pl.BlockSpec((1, LANE), lambda i, h: (0, 0)),
                pl.BlockSpec((1, DH_A), lambda i, h: (0, h)),
                pl.BlockSpec((1, SB, 1, DH_A, DH_A), lambda i, h: (layer, i, h, 0, 0)),
                pl.BlockSpec((None, SB, 1, 1, DH_A), lambda i, h: (layer, i, h, 0, 0)),
                pl.BlockSpec((None, 1, SB, 1), lambda i, h: (layer, h, i, 0))]
    a_specs, a_ops, aliases = _alias_args(c_prev, len(in_specs), 1)
    return pl.pallas_call(
        _mlstm_sample_body,
        grid=(nb, H_A),
        in_specs=in_specs + a_specs,
        out_specs=[pl.BlockSpec((SB, DH_A), lambda i, h: (i, h)),
                   pl.BlockSpec((1, SB, 1, DH_A, DH_A), lambda i, h: (layer, i, h, 0, 0)),
                   pl.BlockSpec((SB, 1, 1, DH_A), lambda i, h: (i, h, 0, 0)),
                   pl.BlockSpec((1, SB, 1), lambda i, h: (h, i, 0))],
        out_shape=[jax.ShapeDtypeStruct((dec_b, W_A), _MXU_DT),
                   jax.ShapeDtypeStruct(c_st.shape, F32),
                   jax.ShapeDtypeStruct((dec_b, H_A, 1, DH_A), F32),
                   jax.ShapeDtypeStruct((H_A, dec_b, 1), F32)],
        input_output_aliases=aliases,
        compiler_params=_cparams("parallel", "parallel"),
        name="mlstm_sample",
    )(proj, proj, proj, proj, proj, bias_row, nw, c_st, n_st, m_st, *a_ops)


GDN_G = 2
GDN_VT = GDN_G * CHUNK
GDN_W = GDN_G * DH_B
GDN_NG = H_B // GDN_G


def _gdn_gates(sm, alog_row, dtb_row):
    la = -jnp.exp(alog_row) * _softplus(sm + dtb_row)
    beta = _sigmoid(sm)
    return la, beta


def _gdn_post(o, z, nw):
    outs = []
    for h in range(H_B):
        sl = slice(h * DH_B, (h + 1) * DH_B)
        oh = o[:, sl]
        oh = oh * lax.rsqrt(jnp.mean(oh * oh, axis=-1, keepdims=True) + RMS_EPS) * nw
        outs.append(oh * _silu(z[:, sl]))
    return jnp.concatenate(outs, axis=1)


def _l2n_heads(x, scale):
    outs = []
    for h in range(H_B):
        xh = x[:, h * DH_B:(h + 1) * DH_B]
        outs.append(xh * (lax.rsqrt(jnp.sum(xh * xh, axis=-1, keepdims=True) + L2_EPS) * scale))
    return jnp.concatenate(outs, axis=1)


def _gdn_prompt_body(q_ref, k_ref, v_ref, z_ref, sm_ref, cw_ref, alog_ref, dtb_ref, nw_ref,
                     yb_ref, so_ref, sbd_ref, prev_ref):
    ci = pl.program_id(1)

    @pl.when(ci == 0)
    def _():
        sbd_ref[...] = jnp.zeros_like(sbd_ref)
        prev_ref[...] = jnp.zeros_like(prev_ref)

    L = CHUNK
    acts = []
    for p, ref in enumerate((q_ref, k_ref, v_ref)):
        u = ref[...]
        pu = prev_ref[p]
        cw = cw_ref[:, p * W_B:(p + 1) * W_B]
        y = cw[CONV_W - 1:CONV_W, :] * u
        for j in range(1, CONV_W):
            y = y + cw[CONV_W - 1 - j:CONV_W - j, :] * _shift_rows(u, pu, j)
        prev_ref[p] = u
        acts.append(_silu(y))
    qn = _l2n_heads(acts[0], DH_B ** -0.5)
    kn = _l2n_heads(acts[1], 1.0)
    va = acts[2]

    la_all, beta_all = _gdn_gates(sm_ref[...], alog_ref[...], dtb_ref[...])
    gc_all = _cumsum_rows(la_all)

    vt, w = GDN_VT, GDN_W
    ri = lax.broadcasted_iota(jnp.int32, (vt, vt), 0)
    cj = lax.broadcasted_iota(jnp.int32, (vt, vt), 1)
    same = (ri // L) == (cj // L)
    incl = jnp.logical_and(same, cj <= ri)
    strict = jnp.logical_and(same, cj < ri)
    eye = (ri == cj).astype(F32)
    bm = (lax.broadcasted_iota(jnp.int32, (vt, w), 0) // L) == (lax.broadcasted_iota(jnp.int32, (vt, w), 1) // DH_B)
    stack = lambda x: jnp.where(bm, jnp.concatenate([x] * GDN_G, axis=0), 0.0)

    grp = []
    for gi in range(GDN_NG):
        heads = [gi * GDN_G + e for e in range(GDN_G)]
        lsl = slice(gi * w, (gi + 1) * w)
        g_col = jnp.concatenate([gc_all[:, 8 + h:9 + h] for h in heads], axis=0)
        beta_col = jnp.concatenate([beta_all[:, 16 + h:17 + h] for h in heads], axis=0)
        glast_col = jnp.concatenate(
            [jnp.broadcast_to(gc_all[L - 1:L, 8 + h:9 + h], (L, 1)) for h in heads], axis=0)
        eg_state = jnp.concatenate(
            [jnp.broadcast_to(jnp.exp(gc_all[L - 1:L, 8 + h:9 + h]), (DH_B, 1)) for h in heads], axis=0)
        g_row = _col_to_row(g_col)
        k_s = stack(kn[:, lsl])
        q_s = stack(qn[:, lsl])
        v_s = stack(va[:, lsl])
        dec = jnp.exp(jnp.where(incl, g_col - g_row, NEG))
        a_mat = jnp.where(strict, beta_col * _mm_nt(k_s, k_s) * dec, 0.0)
        attn = jnp.where(incl, _mm_nt(q_s, k_s) * dec, 0.0)
        eg_col = jnp.exp(g_col)
        rhs = jnp.concatenate([v_s * beta_col, k_s * (beta_col * eg_col)], axis=1)
        grp.append((a_mat, attn, rhs, q_s * eg_col, k_s * jnp.exp(glast_col - g_col), eg_state))

    t_invs = _tri_inv([-g[0] for g in grp], eye, 6, _mm3)

    o_parts = []
    for gi in range(GDN_NG):
        _, attn, rhs, q_e, k_e, eg_state = grp[gi]
        sol = _mm(t_invs[gi], rhs)
        sbd = sbd_ref[gi]
        v_new = sol[:, :w] - _mm(sol[:, w:], sbd)
        o_m = _mm(q_e, sbd) + _mm(attn, v_new)
        o_g = o_m[0:L]
        for e in range(1, GDN_G):
            o_g = o_g + o_m[e * L:(e + 1) * L]
        o_parts.append(o_g)
        sbd_ref[gi] = eg_state * sbd + _mm_tn(k_e, v_new)

    o = jnp.concatenate(o_parts, axis=1)
    yb_ref[...] = _gdn_post(o, z_ref[...], nw_ref[...]).astype(yb_ref.dtype)

    @pl.when(ci == pl.num_programs(1) - 1)
    def _():
        for h in range(H_B):
            gi, e = divmod(h, GDN_G)
            so_ref[0, h] = sbd_ref[gi, e * DH_B:(e + 1) * DH_B, e * DH_B:(e + 1) * DH_B]


def _gdn_prompt(proj, conv_w, alog_row, dtb_row, nw, bsz, nc):
    ntp = bsz * nc * CHUNK
    col = lambda j: (lambda b, c: (b * nc + c, j))
    wide = OFF_B // W_B
    const = lambda b, c: (0, 0)
    return pl.pallas_call(
        _gdn_prompt_body,
        grid=(bsz, nc),
        in_specs=[pl.BlockSpec((CHUNK, W_B), col(wide)), pl.BlockSpec((CHUNK, W_B), col(wide + 1)),
                  pl.BlockSpec((CHUNK, W_B), col(wide + 2)), pl.BlockSpec((CHUNK, W_B), col(wide + 3)),
                  pl.BlockSpec((CHUNK, LANE), col(OFF_S // LANE)),
                  pl.BlockSpec((CONV_W, 3 * W_B), const),
                  pl.BlockSpec((1, LANE), const), pl.BlockSpec((1, LANE), const),
                  pl.BlockSpec((1, DH_B), const)],
        out_specs=[pl.BlockSpec((CHUNK, W_B), lambda b, c: (b * nc + c, 0)),
                   pl.BlockSpec((1, H_B, DH_B, DH_B), lambda b, c: (b, 0, 0, 0))],
        out_shape=[jax.ShapeDtypeStruct((ntp, W_B), _MXU_DT),
                   jax.ShapeDtypeStruct((bsz, H_B, DH_B, DH_B), F32)],
        scratch_shapes=[pltpu.VMEM((GDN_NG, GDN_W, GDN_W), F32),
                        pltpu.VMEM((3, CHUNK, W_B), F32)],
        compiler_params=_cparams("parallel", "arbitrary"),
        name="gdn_prompt",
    )(proj, proj, proj, proj, proj, conv_w, alog_row, dtb_row, nw)


def _gdn_sample_body(q_ref, k_ref, v_ref, z_ref, sm_ref, bq_ref, bk_ref, bv_ref, cwq_ref, cwk_ref, cwv_ref,
                     alog_ref, dtb_ref, nw_ref, s_ref, *rest):
    yb_ref, so_ref, nq_ref, nk_ref, nv_ref = rest[-5:]
    h = pl.program_id(1)
    acts = []
    for u_ref, b_ref, cw_ref, n_ref in ((q_ref, bq_ref, cwq_ref, nq_ref), (k_ref, bk_ref, cwk_ref, nk_ref),
                                        (v_ref, bv_ref, cwv_ref, nv_ref)):
        u = u_ref[...]
        y = cw_ref[CONV_W - 1:CONV_W, :] * u
        for j in range(CONV_W - 1):
            y = y + cw_ref[j:j + 1, :] * b_ref[j]
        for j in range(CONV_W - 2):
            n_ref[j] = b_ref[j + 1]
        n_ref[CONV_W - 2] = u
        acts.append(_silu(y))
    qa, ka, va = acts
    q = qa * (lax.rsqrt(jnp.sum(qa * qa, axis=-1, keepdims=True) + L2_EPS) * DH_B ** -0.5)
    k = ka * lax.rsqrt(jnp.sum(ka * ka, axis=-1, keepdims=True) + L2_EPS)
    la_all, beta_all = _gdn_gates(sm_ref[...], alog_ref[...], dtb_ref[...])
    eg = jnp.exp(_lane_pick(la_all, 8 + h))
    beta = _lane_pick(beta_all, 16 + h)
    qk = jnp.sum(q * k, axis=1, keepdims=True)
    q_t = jnp.transpose(_pad_rows(q, LANE))
    k_t = jnp.transpose(_pad_rows(k, LANE))
    nw = nw_ref[...]
    z = z_ref[...]
    for b in range(SB):
        rb = slice(b, b + 1)
        s_b = s_ref[0, b, 0]
        k_col = k_t[:, rb]
        q_col = q_t[:, rb]
        ks = jnp.sum(k_col * s_b, axis=0, keepdims=True)
        qs = jnp.sum(q_col * s_b, axis=0, keepdims=True)
        v_new = beta[rb] * va[rb] - (beta[rb] * eg[rb]) * ks
        o = eg[rb] * qs + qk[rb] * v_new
        o = o * lax.rsqrt(jnp.mean(o * o, axis=-1, keepdims=True) + RMS_EPS) * nw
        yb_ref[rb, :] = (o * _silu(z[rb])).astype(yb_ref.dtype)
        so_ref[0, b, 0] = eg[rb] * s_b + k_col * v_new


def _gdn_sample(proj, conv_t, conv_w, alog_row, dtb_row, nw, s_st, s_prev, layer, ntp, nb):
    r0 = ntp // SB
    base = OFF_B // DH_B
    col = lambda j: (lambda i, h: (r0 + i, base + j + h))
    const = lambda i, h: (0, 0)
    dec_b = nb * SB
    buf = lambda p: pl.BlockSpec((None, CONV_W - 1, SB, DH_B), lambda i, h: (layer, 0, i, p * H_B + h))
    cws = lambda p: pl.BlockSpec((CONV_W, DH_B), lambda i, h: (0, p * H_B + h))
    nbuf = pl.BlockSpec((CONV_W - 1, SB, DH_B), lambda i, h: (0, i, h))
    nshape = jax.ShapeDtypeStruct((CONV_W - 1, dec_b, W_B), F32)
    in_specs = [pl.BlockSpec((SB, DH_B), col(0)), pl.BlockSpec((SB, DH_B), col(H_B)),
                pl.BlockSpec((SB, DH_B), col(2 * H_B)), pl.BlockSpec((SB, DH_B), col(3 * H_B)),
                pl.BlockSpec((SB, LANE), lambda i, h: (r0 + i, OFF_S // LANE)),
                buf(0), buf(1), buf(2), cws(0), cws(1), cws(2),
                pl.BlockSpec((1, LANE), const), pl.BlockSpec((1, LANE), const), pl.BlockSpec((1, DH_B), const),
                pl.BlockSpec((1, SB, 1, DH_B, DH_B), lambda i, h: (layer, i, h, 0, 0))]
    a_specs, a_ops, aliases = _alias_args(s_prev, len(in_specs), 1)
    return pl.pallas_call(
        _gdn_sample_body,
        grid=(nb, H_B),
        in_specs=in_specs + a_specs,
        out_specs=[pl.BlockSpec((SB, DH_B), lambda i, h: (i, h)),
                   pl.BlockSpec((1, SB, 1, DH_B, DH_B), lambda i, h: (layer, i, h, 0, 0)),
                   nbuf, nbuf, nbuf],
        out_shape=[jax.ShapeDtypeStruct((dec_b, W_B), _MXU_DT),
                   jax.ShapeDtypeStruct(s_st.shape, F32),
                   nshape, nshape, nshape],
        input_output_aliases=aliases,
        compiler_params=_cparams("parallel", "parallel"),
        name="gdn_sample",
    )(proj, proj, proj, proj, proj, conv_t, conv_t, conv_t, conv_w, conv_w, conv_w,
      alog_row, dtb_row, nw, s_st, *a_ops)


RW_G = 4
RW_VT = RW_G * CHUNK
RW_W = RW_G * DH_C
RW_NG = H_C // RW_G


def _block_ones(n, seg):
    r = lax.broadcasted_iota(jnp.int32, (n, n), 0) // seg
    c = lax.broadcasted_iota(jnp.int32, (n, n), 1) // seg
    return (r == c).astype(_MXU_DT)


def _seg_sum(x, ones_blk):
    w = ones_blk.shape[0]
    parts = [_mm2_lhs(x[:, j * w:(j + 1) * w], ones_blk) for j in range(x.shape[1] // w)]
    return jnp.concatenate(parts, axis=1)


def _rwkv_pre(r_raw, k_raw, v_raw, l_raw, pr, pk, pv, pl_, prm, ones_blk):
    mu, w0, w2, a0, a2, g2, k_k, k_a = prm
    mix = lambda x, p, m: x + (p - x) * m
    r = mix(r_raw, pr, mu[:, 0:W_C])
    k = mix(k_raw, pk, mu[:, W_C:2 * W_C])
    v = mix(v_raw, pv, mu[:, 2 * W_C:3 * W_C])
    lo = mix(l_raw, pl_, mu[:, 3 * W_C:N_C])
    xw = lo[:, 0:R_W + R_A]
    xg = lo[:, R_W + R_A:R_W + R_A + R_G]
    lane = lax.broadcasted_iota(jnp.int32, xw.shape, 1)
    tw = jnp.where(lane < R_W, jnp.tanh(xw), 0.0)
    ta = jnp.where(lane < R_W, 0.0, xw)
    w_log = -_softplus(-(w0 + _mm(tw, w2))) - 0.5
    lw = -jnp.exp(w_log)
    a = _sigmoid(a0 + _mm(ta, a2))
    g = _mm(_sigmoid(xg), g2)
    kk = k * k_k
    kk = kk * lax.rsqrt(_seg_sum(kk * kk, ones_blk) + L2_EPS)
    k2 = k * (1.0 + (a - 1.0) * k_a)
    return r, lw, k2, v, -kk, kk * a, g


def _rwkv_post(y, r, k2, v, g, r_k, ln_w, ln_b, ones_blk):
    mean = _seg_sum(y, ones_blk) * (1.0 / DH_C)
    yc = y - mean
    var = _seg_sum(yc * yc, ones_blk) * (1.0 / DH_C)
    yn = yc * lax.rsqrt(var + GN_EPS) * ln_w + ln_b
    yn = yn + _seg_sum(r * k2 * r_k, ones_blk) * v
    return yn * g


def _rwkv_prompt_body(r_ref, k_ref, v_ref, l_ref, mu_ref, w0_ref, w2_ref, a0_ref, a2_ref, g2_ref,
                      kk_ref, ka_ref, rk_ref, lnw_ref, lnb_ref,
                      yc_ref, so_ref, sbd_ref, pr_ref, pk_ref, pv_ref, pl_ref):
    ci = pl.program_id(1)

    @pl.when(ci == 0)
    def _():
        sbd_ref[...] = jnp.zeros_like(sbd_ref)
        pr_ref[...] = jnp.zeros_like(pr_ref)
        pk_ref[...] = jnp.zeros_like(pk_ref)
        pv_ref[...] = jnp.zeros_like(pv_ref)
        pl_ref[...] = jnp.zeros_like(pl_ref)

    L = CHUNK
    ones_blk = _block_ones(RW_W, DH_C)
    raws = [ref[...] for ref in (r_ref, k_ref, v_ref, l_ref)]
    prevs = [_shift_rows(x, p[...], 1) for x, p in zip(raws, (pr_ref, pk_ref, pv_ref, pl_ref))]
    for x, p in zip(raws, (pr_ref, pk_ref, pv_ref, pl_ref)):
        p[...] = x
    prm = (mu_ref[...], w0_ref[...], w2_ref[...], a0_ref[...], a2_ref[...], g2_ref[...], kk_ref[...], ka_ref[...])
    r, lw, k2, v, a_vec, b_vec, g = _rwkv_pre(*raws, *prevs, prm, ones_blk)

    cl = _cumsum_rows(lw)
    e_incl = jnp.exp(cl)
    e_excl = jnp.exp(cl - lw)
    e_inv = jnp.exp(-cl)
    a_t = a_vec * e_excl
    r_t = r * e_incl
    b_t = b_vec * e_inv
    k_t = k2 * e_inv

    vt, w = RW_VT, RW_W
    ri = lax.broadcasted_iota(jnp.int32, (vt, vt), 0)
    cj = lax.broadcasted_iota(jnp.int32, (vt, vt), 1)
    same = (ri // L) == (cj // L)
    incl = jnp.logical_and(same, cj <= ri)
    strict = jnp.logical_and(same, cj < ri)
    eye = (ri == cj).astype(F32)
    bm = (lax.broadcasted_iota(jnp.int32, (vt, w), 0) // L) == (lax.broadcasted_iota(jnp.int32, (vt, w), 1) // DH_C)
    stack = lambda x: jnp.where(bm, jnp.concatenate([x] * RW_G, axis=0), 0.0)

    grp = []
    for gi in range(RW_NG):
        lsl = slice(gi * w, (gi + 1) * w)
        a_s, r_s, b_s, k_s, v_m = (stack(x[:, lsl]) for x in (a_t, r_t, b_t, k_t, v))
        sbd = sbd_ref[gi]
        ar = jnp.concatenate([a_s, r_s], axis=0)
        bk = jnp.concatenate([b_s, k_s], axis=0)
        gm = _mm_nt(ar, bk)
        ars = _mm_nt(ar, sbd)
        rhs_u = ars[:vt] + _mm(jnp.where(strict, gm[:vt, vt:], 0.0), v_m)
        rbk = jnp.concatenate([jnp.where(incl, gm[vt:, :vt], 0.0), jnp.where(incl, gm[vt:, vt:], 0.0)], axis=1)
        grp.append((jnp.where(strict, gm[:vt, :vt], 0.0), rhs_u, rbk, ars[vt:], v_m, bk, sbd))

    t_invs = _tri_inv([g[0] for g in grp], eye, 6, _mm)

    y_parts = []
    for gi in range(RW_NG):
        _, rhs_u, rbk, rs, v_m, bk, sbd = grp[gi]
        u_m = _mm(t_invs[gi], rhs_u)
        uv = jnp.concatenate([u_m, v_m], axis=0)
        y_m = rs + _mm(rbk, uv)
        y_g = y_m[0:L]
        for e in range(1, RW_G):
            y_g = y_g + y_m[e * L:(e + 1) * L]
        y_parts.append(y_g)
        sbd_ref[gi] = (sbd + _mm_tn(uv, bk)) * e_incl[L - 1:L, gi * w:(gi + 1) * w]

    y = jnp.concatenate(y_parts, axis=1)
    out = _rwkv_post(y, r, k2, v, g, rk_ref[...], lnw_ref[...], lnb_ref[...], ones_blk)
    yc_ref[...] = out.astype(yc_ref.dtype)

    @pl.when(ci == pl.num_programs(1) - 1)
    def _():
        for h in range(H_C):
            gi, e = divmod(h, RW_G)
            so_ref[0, h] = sbd_ref[gi, e * DH_C:(e + 1) * DH_C, e * DH_C:(e + 1) * DH_C]


def _rwkv_param_specs(const):
    row = lambda n: pl.BlockSpec((1, n), const)
    return [row(N_C), row(W_C), pl.BlockSpec((LANE, W_C), const), row(W_C), pl.BlockSpec((LANE, W_C), const),
            pl.BlockSpec((R_G, W_C), const), row(W_C), row(W_C), row(W_C), row(W_C), row(W_C)]


def _rwkv_prompt(proj, params, bsz, nc):
    ntp = bsz * nc * CHUNK
    col = lambda j: (lambda b, c: (b * nc + c, j))
    wide = OFF_C // W_C
    const = lambda b, c: (0, 0)
    lora_w = N_C - 3 * W_C
    return pl.pallas_call(
        _rwkv_prompt_body,
        grid=(bsz, nc),
        in_specs=[pl.BlockSpec((CHUNK, W_C), col(wide)), pl.BlockSpec((CHUNK, W_C), col(wide + 1)),
                  pl.BlockSpec((CHUNK, W_C), col(wide + 2)),
                  pl.BlockSpec((CHUNK, lora_w), col((OFF_C + 3 * W_C) // lora_w))] + _rwkv_param_specs(const),
        out_specs=[pl.BlockSpec((CHUNK, W_C), lambda b, c: (b * nc + c, 0)),
                   pl.BlockSpec((1, H_C, DH_C, DH_C), lambda b, c: (b, 0, 0, 0))],
        out_shape=[jax.ShapeDtypeStruct((ntp, W_C), _MXU_DT),
                   jax.ShapeDtypeStruct((bsz, H_C, DH_C, DH_C), F32)],
        scratch_shapes=[pltpu.VMEM((RW_NG, RW_W, RW_W), F32),
                        pltpu.VMEM((CHUNK, W_C), F32), pltpu.VMEM((CHUNK, W_C), F32),
                        pltpu.VMEM((CHUNK, W_C), F32), pltpu.VMEM((CHUNK, lora_w), F32)],
        compiler_params=_cparams("parallel", "arbitrary"),
        name="rwkv_prompt",
    )(proj, proj, proj, proj, *params)


def _rwkv_sample_body(r_ref, k_ref, v_ref, l_ref, sh_ref, mu_ref, w0_ref, w2_ref, a0_ref, a2_ref, g2_ref,
                      kk_ref, ka_ref, rk_ref, lnw_ref, lnb_ref, s_ref, *rest):
    yc_ref, so_ref, tr_ref, row_ref, yt_ref = rest[-5:]
    h = pl.program_id(1)
    ones_blk = _block_ones(RW_W, DH_C)

    @pl.when(h == 0)
    def _():
        raws = [ref[...] for ref in (r_ref, k_ref, v_ref, l_ref)]
        sh = sh_ref[0]
        prevs = [sh[:, 0:W_C], sh[:, W_C:2 * W_C], sh[:, 2 * W_C:3 * W_C], sh[:, 3 * W_C:N_C]]
        prm = (mu_ref[...], w0_ref[...], w2_ref[...], a0_ref[...], a2_ref[...], g2_ref[...], kk_ref[...],
               ka_ref[...])
        r, lw, k2, v, a_vec, b_vec, g = _rwkv_pre(*raws, *prevs, prm, ones_blk)
        for i, x in enumerate((r, jnp.exp(lw), k2, a_vec, b_vec, v)):
            tr_ref[i] = jnp.transpose(x)
        for i, x in enumerate((r, k2, v, g)):
            row_ref[i] = x

    off = pl.multiple_of(h * DH_C, DH_C)
    r_t, w_t, k_t, a_t, b_t = (tr_ref[i, pl.ds(off, DH_C), :] for i in range(5))

    def vstep(vi, carry):
        s = s_ref[0, 0, vi]
        sa = jnp.sum(s * a_t, axis=0, keepdims=True)
        s_new = s * w_t + sa * b_t + tr_ref[5, pl.ds(off + vi, 1), :] * k_t
        so_ref[0, 0, vi] = s_new
        yt_ref[pl.ds(off + vi, 1), :] = jnp.sum(s_new * r_t, axis=0, keepdims=True)
        return carry

    lax.fori_loop(0, DH_C, vstep, 0, unroll=4)

    @pl.when(h == pl.num_programs(1) - 1)
    def _():
        y = jnp.transpose(yt_ref[...])
        out = _rwkv_post(y, row_ref[0], row_ref[1], row_ref[2], row_ref[3], rk_ref[...], lnw_ref[...],
                         lnb_ref[...], ones_blk)
        yc_ref[...] = out.astype(yc_ref.dtype)


def _rwkv_sample(proj, shift_st, params, s_st, s_prev, layer, ntp):
    dec_b = s_st.shape[-1]
    bb = LANE
    assert dec_b % bb == 0 and ntp % bb == 0
    r0 = ntp // bb
    col = lambda j: (lambda i, h: (r0 + i, j))
    wide = OFF_C // W_C
    const = lambda i, h: (0, 0)
    lora_w = N_C - 3 * W_C
    st_spec = pl.BlockSpec((1, 1, DH_C, DH_C, bb), lambda i, h: (layer, h, 0, 0, i))
    in_specs = ([pl.BlockSpec((bb, W_C), col(wide)), pl.BlockSpec((bb, W_C), col(wide + 1)),
                 pl.BlockSpec((bb, W_C), col(wide + 2)),
                 pl.BlockSpec((bb, lora_w), col((OFF_C + 3 * W_C) // lora_w)),
                 pl.BlockSpec((1, bb, N_C), lambda i, h: (layer, i, 0))] + _rwkv_param_specs(const) + [st_spec])
    a_specs, a_ops, aliases = _alias_args(s_prev, len(in_specs), 1)
    return pl.pallas_call(
        _rwkv_sample_body,
        grid=(dec_b // bb, H_C),
        in_specs=in_specs + a_specs,
        out_specs=[pl.BlockSpec((bb, W_C), lambda i, h: (i, 0)), st_spec],
        out_shape=[jax.ShapeDtypeStruct((dec_b, W_C), _MXU_DT),
                   jax.ShapeDtypeStruct(s_st.shape, F32)],
        scratch_shapes=[pltpu.VMEM((6, W_C, bb), F32), pltpu.VMEM((4, bb, W_C), F32), pltpu.VMEM((W_C, bb), F32)],
        input_output_aliases=aliases,
        compiler_params=_cparams("parallel", "arbitrary"),
        name="rwkv_sample",
    )(proj, proj, proj, proj, shift_st, *params, s_st, *a_ops)


def _lane_row(pairs):
    row = jnp.zeros((LANE,), F32)
    for off, vec in pairs:
        row = lax.dynamic_update_slice(row, vec.astype(F32), (off,))
    return row[None]


def _prep_w_in(w):
    a0, b0, c0, g0 = 0, N_A, N_A + N_B, N_A + N_B + N_C
    parts = [w[..., g0:g0 + N_GATE], w[..., a0:a0 + 4 * W_A], w[..., b0:b0 + 4 * W_B], w[..., c0:c0 + N_C],
             w[..., a0 + 4 * W_A:a0 + N_A], w[..., b0 + 4 * W_B:b0 + N_B]]
    parts = [_cast(p) for p in parts]
    used = sum(p.shape[-1] for p in parts)
    parts.append(jnp.zeros(w.shape[:-1] + (N_PROJ - used,), _MXU_DT))
    return jnp.concatenate(parts, axis=-1)


def kernel(x_prompt, x_sample, state_mlstm_c, state_mlstm_n, state_mlstm_m, state_gdn_s, state_gdn_conv,
           state_rwkv_s, state_rwkv_shift, meta_tokens, norm_mix_w, w_in, mlstm_b_i, mlstm_b_f, mlstm_norm_w,
           gdn_conv_w, gdn_a_log, gdn_dt_bias, gdn_norm_w, rwkv_mu, rwkv_w0, rwkv_w2, rwkv_a0, rwkv_a2,
           rwkv_g2, rwkv_k_k, rwkv_k_a, rwkv_r_k, rwkv_ln_w, rwkv_ln_b, w_branch_a, w_branch_b, w_branch_c,
           w_out, norm_mlp_w, w_up, w_down, final_norm_w):
    bsz, seq, _ = x_prompt.shape
    dec_b = x_sample.shape[0]
    depth = w_in.shape[0]
    assert x_sample.shape[1] == 1 and seq % CHUNK == 0 and dec_b % SB == 0
    t_pad = LEAD_PAD + N_META + seq
    nc = t_pad // CHUNK
    ntp = bsz * t_pad
    nb = dec_b // SB
    assert ntp % SB == 0

    lead = jnp.concatenate([jnp.zeros((LEAD_PAD, D_MODEL), F32), meta_tokens.astype(F32)], axis=0)
    pieces = []
    for b in range(bsz):
        pieces += [lead, x_prompt[b]]
    x = jnp.concatenate(pieces + [x_sample[:, 0]], axis=0)

    rows = jnp.arange(ntp + dec_b)
    keep = jnp.logical_or(rows >= ntp, (rows % t_pad) >= LEAD_PAD).astype(F32)[:, None]

    n_st = state_mlstm_n[:, :, :, None, :]
    m_st = jnp.swapaxes(state_mlstm_m, 1, 2)[..., None]
    conv_t = jnp.swapaxes(state_gdn_conv, 1, 2)
    rwkv_st = jnp.transpose(state_rwkv_s, (0, 2, 3, 4, 1))
    row = lambda v: v.reshape(1, -1).astype(F32)
    w_in_c = _prep_w_in(w_in)
    wa_c, wb_c, wc_c, w_out_c, w_up_c, w_down_c = (_cast(w) for w in (w_branch_a, w_branch_b, w_branch_c,
                                                                      w_out, w_up, w_down))

    outs = [[] for _ in range(14)]
    s_c = s_gs = s_rs = None
    for l in range(depth):
        proj = _mm_rms(x, row(norm_mix_w[l]), w_in_c, l, F32, None, 1280, "mm_in")
        last_rows = lambda n, c0, c1: jnp.stack(
            [proj[(b + 1) * t_pad - n:(b + 1) * t_pad, c0:c1] for b in range(bsz)])

        bias_row = _lane_row([(0, mlstm_b_i[l]), (H_A, mlstm_b_f[l])])
        nw_a = row(mlstm_norm_w[l])
        ya_p, pc_, pn_, pm_ = _mlstm_prompt(proj, bias_row, nw_a, bsz, nc)
        ya_s, s_c, sn_, sm_ = _mlstm_sample(proj, bias_row, nw_a, state_mlstm_c, n_st, m_st, s_c, l, ntp, nb)

        alog_row = _lane_row([(8, gdn_a_log[l])])
        dtb_row = _lane_row([(8, gdn_dt_bias[l])])
        nw_b = row(gdn_norm_w[l])
        conv_w = gdn_conv_w[l].astype(F32)
        yb_p, pgs_ = _gdn_prompt(proj, conv_w, alog_row, dtb_row, nw_b, bsz, nc)
        yb_s, s_gs, nq_, nk_, nv_ = _gdn_sample(proj, conv_t, conv_w, alog_row, dtb_row, nw_b, state_gdn_s,
                                                s_gs, l, ntp, nb)
        pgconv_ = last_rows(CONV_W - 1, OFF_B, OFF_B + 3 * W_B)
        sgconv_ = jnp.swapaxes(jnp.concatenate([nq_, nk_, nv_], axis=-1), 0, 1)

        w2p = jnp.zeros((LANE, W_C), F32).at[0:R_W].set(rwkv_w2[l])
        a2p = jnp.zeros((LANE, W_C), F32).at[R_W:R_W + R_A].set(rwkv_a2[l])
        params = (row(rwkv_mu[l]), row(rwkv_w0[l]), w2p, row(rwkv_a0[l]), a2p, rwkv_g2[l].astype(F32),
                  row(rwkv_k_k[l]), row(rwkv_k_a[l]), row(rwkv_r_k[l]), row(rwkv_ln_w[l]), row(rwkv_ln_b[l]))
        yc_p, prs_ = _rwkv_prompt(proj, params, bsz, nc)
        yc_s, s_rs = _rwkv_sample(proj, state_rwkv_shift, params, rwkv_st, s_rs, l, ntp)
        prshift_ = last_rows(1, OFF_C, OFF_C + N_C)[:, 0]
        srshift_ = proj[ntp:, OFF_C:OFF_C + N_C]

        ya = jnp.concatenate([ya_p, ya_s], axis=0)
        yb = jnp.concatenate([yb_p, yb_s], axis=0)
        yc = jnp.concatenate([yc_p, yc_s], axis=0)
        merged = _mm_merge(ya, yb, yc, wa_c, wb_c, wc_c, l, proj)
        x = _mm_res(merged, w_out_c, l, x, keep, name="mm_out")
        u2 = _mm_rms(x, row(norm_mlp_w[l]), w_up_c, l, _MXU_DT, "relu2", 1024, "mm_up")
        x = _mm_res(u2, w_down_c, l, x, keep, name="mm_down")

        for acc, val in zip(outs, (pc_, pn_[:, :, 0], pm_[:, :, 0, 0], pgs_, pgconv_, prs_, prshift_,
                                   None, sn_[:, :, 0], jnp.swapaxes(sm_[..., 0], 0, 1), None, sgconv_, None,
                                   srshift_)):
            acc.append(val)

    y = _final_rms(x, row(final_norm_w))
    y_prompt = y[:ntp].reshape(bsz, t_pad, D_MODEL)[:, CHUNK:]
    y_sample = y[ntp:][:, None]
    stacked = {7: s_c, 10: s_gs, 12: jnp.transpose(s_rs, (0, 4, 1, 2, 3))}
    return (y_prompt, y_sample) + tuple(stacked[i] if i in stacked else jnp.stack(acc)
                                        for i, acc in enumerate(outs))
```
